```python
import math
import jax, jax.numpy as jnp
from jax import lax
import numpy as np

D_MODEL = 1024
BATCH = 2
SEQ = 8192
DEPTH = 1

HEAD_DIM = 64
N_Q_HEADS = 8
N_KV_HEADS = 2
Q_PER_KV = N_Q_HEADS // N_KV_HEADS
ATTN_WIDTH = N_Q_HEADS * HEAD_DIM
KV_WIDTH = N_KV_HEADS * HEAD_DIM
WINDOW = 128
ATTN_BLOCK = 128
N_BUCKETS = 32
MAX_DISTANCE = 128
GMLP_GROUPS = 8
GMLP_GROUP_DIM = 64
GMLP_WIDTH = GMLP_GROUPS * GMLP_GROUP_DIM
GMLP_CHUNK = 128
N_BRANCH = 2
IN_WIDTH = ATTN_WIDTH + 2 * KV_WIDTH + 2 * GMLP_WIDTH + N_BRANCH * D_MODEL
PEER_HEADS = 8
PEER_NKEYS = 128
PEER_N_EXPERTS = PEER_NKEYS * PEER_NKEYS
PEER_KEY_DIM = 256
PEER_HALF = PEER_KEY_DIM // 2
PEER_TOPK = 16
PEER_TOKEN_BLOCK = 128
EPS = 1e-6
NEG_INF = -1e30

kernel_name = "hybrid_swa_gmlp_peer_block"


def rmsnorm(x, w):
    xf = x.astype(jnp.float32)
    y = xf * lax.rsqrt(jnp.mean(xf * xf, axis=-1, keepdims=True) + EPS)
    return (y * w.astype(jnp.float32)).astype(x.dtype)


def layernorm(x, w, b):
    xf = x.astype(jnp.float32)
    mu = jnp.mean(xf, axis=-1, keepdims=True)
    xc = xf - mu
    y = xc * lax.rsqrt(jnp.mean(xc * xc, axis=-1, keepdims=True) + EPS)
    return (y * w.astype(jnp.float32) + b.astype(jnp.float32)).astype(x.dtype)


def t5_causal_bucket(dist):
    n = np.maximum(dist, 0)
    max_exact = N_BUCKETS // 2
    large = max_exact + (np.log(np.maximum(n, 1) / max_exact) / np.log(MAX_DISTANCE / max_exact)
                         * (N_BUCKETS - max_exact)).astype(np.int32)
    large = np.minimum(large, N_BUCKETS - 1)
    return np.where(n < max_exact, n, large).astype(np.int32)


def sliding_window_gqa(q, k, v, rel_bias, sinks):
    B, S = q.shape[0], q.shape[1]
    nb = S // ATTN_BLOCK
    q = q.reshape(B, nb, ATTN_BLOCK, N_KV_HEADS, Q_PER_KV, HEAD_DIM)
    k = k.reshape(B, nb, ATTN_BLOCK, N_KV_HEADS, HEAD_DIM)
    v = v.reshape(B, nb, ATTN_BLOCK, N_KV_HEADS, HEAD_DIM)

    def band(t):
        prev = jnp.pad(t, ((0, 0), (1, 0), (0, 0), (0, 0), (0, 0)))[:, :-1]
        return jnp.concatenate([prev, t], axis=2)

    kb, vb = band(k), band(v)
    logits = jnp.einsum('bnqhgd,bnshd->bnhgqs', q, kb).astype(jnp.float32) * (HEAD_DIM ** -0.5)

    qi = np.arange(ATTN_BLOCK)[:, None]
    kj = np.arange(2 * ATTN_BLOCK)[None, :]
    dist = qi + ATTN_BLOCK - kj
    bias = rel_bias.astype(jnp.float32)[t5_causal_bucket(dist)]
    bias = bias.transpose(2, 0, 1).reshape(N_KV_HEADS, Q_PER_KV, ATTN_BLOCK, 2 * ATTN_BLOCK)
    in_window = (dist >= 0) & (dist < WINDOW)
    key_exists = (np.arange(nb)[:, None, None] > 0) | (kj >= ATTN_BLOCK)[None]
    valid = in_window[None] & key_exists
    logits = jnp.where(valid[None, :, None, None], logits + bias, NEG_INF)

    sink = jnp.broadcast_to(sinks.astype(jnp.float32).reshape(N_KV_HEADS, Q_PER_KV)[None, None, :, :, None, None],
                            logits.shape[:-1] + (1,))
    probs = jax.nn.softmax(jnp.concatenate([logits, sink], axis=-1), axis=-1)[..., :-1]
    out = jnp.einsum('bnhgqs,bnshd->bnqhgd', probs.astype(vb.dtype), vb)
    return out.reshape(B, S, ATTN_WIDTH)


def chunked_spatial_gating(u, v, ln_w, ln_b, w_s, b_s):
    B, S = u.shape[0], u.shape[1]
    nc = S // GMLP_CHUNK
    v = layernorm(v, ln_w, ln_b).reshape(B, nc, GMLP_CHUNK, GMLP_GROUPS, GMLP_GROUP_DIM)
    causal = np.tril(np.ones((GMLP_CHUNK, GMLP_CHUNK), dtype=bool))
    w = jnp.where(causal[None], w_s, 0).astype(v.dtype)
    s = jnp.einsum('gts,bcsgd->bctgd', w, v) + b_s.T.astype(v.dtype)[None, None, :, :, None]
    y = u.reshape(B, nc, GMLP_CHUNK, GMLP_GROUPS, GMLP_GROUP_DIM) * s
    return y.reshape(B, S, GMLP_WIDTH)


def peer_ffn(x, w_q, keys_1, keys_2, u_tab, v_tab):
    B, S, D = x.shape
    q = (x @ w_q).reshape(B, S, PEER_HEADS, PEER_KEY_DIM)
    s1 = jnp.einsum('bshd,nd->bshn', q[..., :PEER_HALF], keys_1).astype(jnp.float32)
    s2 = jnp.einsum('bshd,nd->bshn', q[..., PEER_HALF:], keys_2).astype(jnp.float32)
    v1, i1 = lax.top_k(s1, PEER_TOPK)
    v2, i2 = lax.top_k(s2, PEER_TOPK)
    cand_s = (v1[..., :, None] + v2[..., None, :]).reshape(B, S, PEER_HEADS, PEER_TOPK * PEER_TOPK)
    cand_i = (i1[..., :, None] * PEER_NKEYS + i2[..., None, :]).reshape(B, S, PEER_HEADS, PEER_TOPK * PEER_TOPK)
    top_s, pos = lax.top_k(cand_s, PEER_TOPK)
    idx = jnp.take_along_axis(cand_i, pos, axis=-1)
    gate = jax.nn.softmax(top_s, axis=-1).astype(x.dtype)

    n_blk = (B * S) // PEER_TOKEN_BLOCK
    xs = x.reshape(n_blk, PEER_TOKEN_BLOCK, D)
    idxs = idx.reshape(n_blk, PEER_TOKEN_BLOCK, PEER_HEADS, PEER_TOPK)
    gates = gate.reshape(n_blk, PEER_TOKEN_BLOCK, PEER_HEADS, PEER_TOPK)

    def token_block(args):
        xb, ib, gb = args
        ue = u_tab[ib]
        act = jax.nn.gelu(jnp.einsum('thkd,td->thk', ue, xb)) * gb
        ve = v_tab[ib]
        return jnp.einsum('thk,thkd->td', act, ve)

    y = lax.map(token_block, (xs, idxs, gates))
    return y.reshape(B, S, D)


def setup_inputs(seed: int = 0) -> dict:
    key = jax.random.key(seed)
    ks = jax.random.split(key, 20)
    f32 = jnp.float32
    L, D = DEPTH, D_MODEL
    nrm = lambda k, shape, scale: jax.random.normal(k, shape, f32) * scale
    return {
        "x": nrm(ks[0], (BATCH, SEQ, D), 1.0),
        "mix_norm_w": 1.0 + nrm(ks[1], (L, D), 0.02),
        "w_in": nrm(ks[2], (L, D, IN_WIDTH), D ** -0.5),
        "rel_bias": nrm(ks[3], (N_BUCKETS, N_Q_HEADS), 0.3),
        "attn_sinks": nrm(ks[4], (L, N_Q_HEADS), 0.5),
        "w_branch_attn": nrm(ks[5], (L, ATTN_WIDTH, D), ATTN_WIDTH ** -0.5),
        "gmlp_ln_w": 1.0 + nrm(ks[6], (L, GMLP_WIDTH), 0.02),
        "gmlp_ln_b": nrm(ks[7], (L, GMLP_WIDTH), 0.02),
        "gmlp_w_s": nrm(ks[8], (L, GMLP_GROUPS, GMLP_CHUNK, GMLP_CHUNK), GMLP_CHUNK ** -0.5),
        "gmlp_b_s": 1.0 + nrm(ks[9], (L, GMLP_GROUPS, GMLP_CHUNK), 0.1),
        "w_branch_gmlp": nrm(ks[10], (L, GMLP_WIDTH, D), GMLP_WIDTH ** -0.5),
        "w_out": nrm(ks[11], (L, D, D), D ** -0.5),
        "ffn_norm_w": 1.0 + nrm(ks[12], (L, D), 0.02),
        "peer_w_q": nrm(ks[13], (L, D, PEER_HEADS * PEER_KEY_DIM), D ** -0.5),
        "peer_keys_1": nrm(ks[14], (L, PEER_NKEYS, PEER_HALF), PEER_HALF ** -0.5),
        "peer_keys_2": nrm(ks[15], (L, PEER_NKEYS, PEER_HALF), PEER_HALF ** -0.5),
        "peer_u": nrm(ks[16], (L, PEER_N_EXPERTS, D), D ** -0.5),
        "peer_v": nrm(ks[17], (L, PEER_N_EXPERTS, D), PEER_HEADS ** -0.5),
        "final_norm_w": 1.0 + nrm(ks[18], (D,), 0.02),
    }


def reference(x, mix_norm_w, w_in, rel_bias, attn_sinks, w_branch_attn, gmlp_ln_w, gmlp_ln_b,
              gmlp_w_s, gmlp_b_s, w_branch_gmlp, w_out, ffn_norm_w, peer_w_q, peer_keys_1,
              peer_keys_2, peer_u, peer_v, final_norm_w):
    B, S, D = x.shape
    splits = np.cumsum([ATTN_WIDTH, KV_WIDTH, KV_WIDTH, GMLP_WIDTH, GMLP_WIDTH, D_MODEL])
    for l in range(DEPTH):
        h = rmsnorm(x, mix_norm_w[l])
        z = h @ w_in[l]
        q, k, v, gu, gv, gate_a, gate_b = jnp.split(z, splits, axis=-1)
        y_a = sliding_window_gqa(q.reshape(B, S, N_Q_HEADS, HEAD_DIM),
                                 k.reshape(B, S, N_KV_HEADS, HEAD_DIM),
                                 v.reshape(B, S, N_KV_HEADS, HEAD_DIM),
                                 rel_bias, attn_sinks[l]) @ w_branch_attn[l]
        y_b = chunked_spatial_gating(jax.nn.gelu(gu), jax.nn.gelu(gv), gmlp_ln_w[l], gmlp_ln_b[l],
                                     gmlp_w_s[l], gmlp_b_s[l]) @ w_branch_gmlp[l]
        merged = jax.nn.sigmoid(gate_a) * y_a + jax.nn.sigmoid(gate_b) * y_b
        x = x + merged @ w_out[l]
        x = x + peer_ffn(rmsnorm(x, ffn_norm_w[l]), peer_w_q[l], peer_keys_1[l], peer_keys_2[l],
                         peer_u[l], peer_v[l])
    return rmsnorm(x, final_norm_w)
```

```python
import functools

import numpy as np
import jax
import jax.numpy as jnp
from jax import lax
from jax.experimental import pallas as pl
from jax.experimental.pallas import tpu as pltpu

F32 = jnp.float32
BF16 = jnp.bfloat16

D_MODEL = 1024
HEAD_DIM = 64
N_Q_HEADS = 8
N_KV_HEADS = 2
Q_PER_KV = N_Q_HEADS // N_KV_HEADS
ATTN_WIDTH = N_Q_HEADS * HEAD_DIM
KV_WIDTH = N_KV_HEADS * HEAD_DIM
WINDOW = 128
BLK = 128
N_BUCKETS = 32
MAX_DISTANCE = 128
GMLP_GROUPS = 8
GMLP_WIDTH = 512
PEER_HEADS = 8
PEER_NKEYS = 128
PEER_N_EXPERTS = PEER_NKEYS * PEER_NKEYS
PEER_HALF = 128
PEER_TOPK = 16
EPS = 1e-6
NEG_INF = -1e30

_C_Q, _C_K, _C_V, _C_GU, _C_GV, _C_GA, _C_GB, _C_END = 0, 512, 640, 768, 1280, 1792, 2816, 3840

MIX_TM = 512
PREP_TT = 512
PEER_TT = 512
PEER_EC = 512
VMEM_LIMIT = 56 * 1024 * 1024


def _rms(x, w):
    return x * lax.rsqrt(jnp.mean(x * x, axis=-1, keepdims=True) + EPS) * w


def _mixer_kernel(sink_ref, x_ref, nw_ref, win_ref, bias_ref, wba_ref, lnw_ref, lnb_ref, ws_ref,
                  btab_ref, wbg_ref, wout_ref, fnw_ref, x1_ref, xn_ref,
                  q_scr, kext, vext, gu_scr, gv_scr, ya_scr, yb_scr):
    j = pl.program_id(1)
    nb = MIX_TM // BLK
    x = x_ref[0]
    h = _rms(x, nw_ref[...]).astype(BF16)

    def proj(lo, hi):
        return jnp.dot(h, win_ref[:, lo:hi], preferred_element_type=F32)

    q_scr[...] = proj(_C_Q, _C_K) * (HEAD_DIM ** -0.5)

    @pl.when(j == 0)
    def _():
        kext[0:BLK, :] = jnp.zeros((BLK, KV_WIDTH), BF16)
        vext[0:BLK, :] = jnp.zeros((BLK, KV_WIDTH), BF16)

    kext[BLK:, :] = proj(_C_K, _C_V).astype(BF16)
    vext[BLK:, :] = proj(_C_V, _C_GU).astype(BF16)
    gu_scr[...] = jax.nn.gelu(proj(_C_GU, _C_GV))
    gv = jax.nn.gelu(proj(_C_GV, _C_GA))
    mu = jnp.mean(gv, axis=-1, keepdims=True)
    gc = gv - mu
    gvn = gc * lax.rsqrt(jnp.mean(gc * gc, axis=-1, keepdims=True) + EPS)
    gv_scr[...] = (gvn * lnw_ref[...] + lnb_ref[...]).astype(BF16)

    lane = lax.broadcasted_iota(jnp.int32, (BLK, BLK), 1)
    low = lane < HEAD_DIM
    col = lax.broadcasted_iota(jnp.int32, (BLK, 2 * BLK), 1)

    def block(b, carry):
        r0 = pl.multiple_of(b * BLK, BLK)
        qb = q_scr[pl.ds(r0, BLK), :]
        kb = kext[pl.ds(r0, 2 * BLK), :]
        vb = vext[pl.ds(r0, 2 * BLK), :]
        parts = []
        for jj in range(4):
            p_j = qb[:, jj * BLK:(jj + 1) * BLK]
            parts.append(jnp.where(low, p_j, 0.0).astype(BF16))
            parts.append(jnp.where(low, 0.0, p_j).astype(BF16))
        qs = jnp.concatenate(parts, axis=0)
        logits = lax.dot_general(qs, kb, (((1,), (1,)), ((), ())), preferred_element_type=F32)
        left_cut = jnp.where(j * nb + b == 0, BLK, 0)
        outs = []
        for r in range(8):
            lg = logits[r * BLK:(r + 1) * BLK, :] + bias_ref[r * BLK:(r + 1) * BLK, :]
            lg = jnp.where(col < left_cut, NEG_INF, lg)
            s = sink_ref[r]
            m = jnp.maximum(jnp.max(lg, axis=-1, keepdims=True), s)
            p = jnp.exp(lg - m)
            den = jnp.sum(p, axis=-1, keepdims=True) + jnp.exp(s - m)
            pv = jnp.dot(p.astype(BF16), vb, preferred_element_type=F32)
            outs.append(pv / den)
        pairs = [jnp.where(low, outs[2 * jj], outs[2 * jj + 1]) for jj in range(4)]
        ya_scr[pl.ds(r0, BLK), :] = jnp.concatenate(pairs, axis=1).astype(BF16)

        gub = gu_scr[pl.ds(r0, BLK), :]
        vnb = gv_scr[pl.ds(r0, BLK), :]
        ys = []
        for mm in range(4):
            vblk = vnb[:, mm * BLK:(mm + 1) * BLK]
            s0 = jnp.dot(ws_ref[2 * mm], vblk, preferred_element_type=F32)
            s1 = jnp.dot(ws_ref[2 * mm + 1], vblk, preferred_element_type=F32)
            sg = jnp.where(low, s0, s1) + btab_ref[:, mm * BLK:(mm + 1) * BLK]
            ys.append(gub[:, mm * BLK:(mm + 1) * BLK] * sg)
        yb_scr[pl.ds(r0, BLK), :] = jnp.concatenate(ys, axis=1).astype(BF16)
        return carry

    lax.fori_loop(0, nb, block, 0)

    kext[0:BLK, :] = kext[MIX_TM:MIX_TM + BLK, :]
    vext[0:BLK, :] = vext[MIX_TM:MIX_TM + BLK, :]

    ya_p = jnp.dot(ya_scr[...], wba_ref[...], preferred_element_type=F32)
    yb_p = jnp.dot(yb_scr[...], wbg_ref[...], preferred_element_type=F32)
    merged = jax.nn.sigmoid(proj(_C_GA, _C_GB)) * ya_p + jax.nn.sigmoid(proj(_C_GB, _C_END)) * yb_p
    x1 = x + jnp.dot(merged.astype(BF16), wout_ref[...], preferred_element_type=F32)
    x1_ref[0] = x1
    xn_ref[0] = _rms(x1, fnw_ref[...]).astype(BF16)


def _const_spec(shape):
    nd = len(shape)
    return pl.BlockSpec(shape, lambda *_: (0,) * nd)


def _mixer(x, sinks, nw, win, bias, wba, lnw, lnb, ws, btab, wbg, wout, fnw):
    B, S, D = x.shape
    grid = (B, S // MIX_TM)
    tile = pl.BlockSpec((1, MIX_TM, D), lambda b, j: (b, j, 0))
    in_specs = [
        pl.BlockSpec(memory_space=pltpu.SMEM),
        tile,
        _const_spec(nw.shape), _const_spec(win.shape), _const_spec(bias.shape), _const_spec(wba.shape),
        _const_spec(lnw.shape), _const_spec(lnb.shape), _const_spec(ws.shape), _const_spec(btab.shape),
        _const_spec(wbg.shape), _const_spec(wout.shape), _const_spec(fnw.shape),
    ]
    return pl.pallas_call(
        _mixer_kernel,
        grid=grid,
        in_specs=in_specs,
        out_specs=[tile, tile],
        out_shape=[jax.ShapeDtypeStruct((B, S, D), F32), jax.ShapeDtypeStruct((B, S, D), BF16)],
        scratch_shapes=[
            pltpu.VMEM((MIX_TM, ATTN_WIDTH), F32),
            pltpu.VMEM((MIX_TM + BLK, KV_WIDTH), BF16),
            pltpu.VMEM((MIX_TM + BLK, KV_WIDTH), BF16),
            pltpu.VMEM((MIX_TM, GMLP_WIDTH), F32),
            pltpu.VMEM((MIX_TM, GMLP_WIDTH), BF16),
            pltpu.VMEM((MIX_TM, ATTN_WIDTH), BF16),
            pltpu.VMEM((MIX_TM, GMLP_WIDTH), BF16),
        ],
        compiler_params=pltpu.CompilerParams(
            dimension_semantics=("arbitrary", "arbitrary"), vmem_limit_bytes=VMEM_LIMIT),
        name="mixer",
    )(sinks, x, nw, win, bias, wba, lnw, lnb, ws, btab, wbg, wout, fnw)


def _cmpx(a, b):
    if a is None:
        return b, None
    if b is None:
        return a, None
    return jnp.maximum(a, b), jnp.minimum(a, b)


def _bitonic_sort_desc(v):
    v = list(v)
    n = len(v)
    k = 2
    while k <= n:
        jj = k // 2
        while jj >= 1:
            for i in range(n):
                l = i ^ jj
                if l > i:
                    hi, lo = _cmpx(v[i], v[l])
                    if (i & k) == 0:
                        v[i], v[l] = hi, lo
                    else:
                        v[i], v[l] = lo, hi
            jj //= 2
        k *= 2
    return v


def _merge_top(a, b):
    n = len(a)
    c = [_cmpx(a[i], b[n - 1 - i])[0] for i in range(n)]
    jj = n // 2
    while jj >= 1:
        for i in range(n):
            if (i & jj) == 0:
                c[i], c[i + jj] = _cmpx(c[i], c[i + jj])
        jj //= 2
    return c


def _top_sorted(vals, k):
    vals = list(vals)
    while len(vals) % k:
        vals.append(None)
    groups = [_bitonic_sort_desc(vals[g:g + k]) for g in range(0, len(vals), k)]
    while len(groups) > 1:
        nxt = [_merge_top(groups[g], groups[g + 1]) for g in range(0, len(groups) - 1, 2)]
        if len(groups) % 2:
            nxt.append(groups[-1])
        groups = nxt
    return groups[0]


def _max_below(vals, bound):
    m = None
    for v in vals:
        w = jnp.where(v < bound, v, -3.0e38)
        m = w if m is None else jnp.maximum(m, w)
    return m


def _prep_kernel(xn_ref, wqt_ref, k1_ref, k2kh_ref, k2hk_ref, thr_ref, c_ref, s2_ref, e2_ref,
                 s1_scr, s2kh_scr, b0_scr):
    nlb = PREP_TT // BLK
    xn = xn_ref[...]
    qt = lax.dot_general(wqt_ref[...], xn, (((1,), (1,)), ((), ())), preferred_element_type=F32)
    nrow = PEER_HEADS * PEER_HALF
    q1 = qt[:nrow].astype(BF16)
    q2 = qt[nrow:].astype(BF16)
    s1 = jnp.dot(k1_ref[...], q1, preferred_element_type=F32)
    s2kh = jnp.dot(k2kh_ref[...], q2, preferred_element_type=F32)
    s2hk = jnp.dot(k2hk_ref[...], q2, preferred_element_type=F32)
    for lb in range(nlb):
        s1_scr[lb] = s1[:, lb * BLK:(lb + 1) * BLK]
        s2kh_scr[lb] = s2kh[:, lb * BLK:(lb + 1) * BLK]
        s2_ref[lb] = s2hk[:, lb * BLK:(lb + 1) * BLK]

    def lane_block(lb, carry):
        s1_rows = [s1_scr[lb, k * 8:(k + 1) * 8, :] for k in range(PEER_NKEYS)]
        s2_rows = [s2kh_scr[lb, k * 8:(k + 1) * 8, :] for k in range(PEER_NKEYS)]
        a = _top_sorted(s1_rows, PEER_TOPK)
        b = _top_sorted(s2_rows, PEER_TOPK)
        a_next = _max_below(s1_rows, a[PEER_TOPK - 1])
        b_next = _max_below(s2_rows, b[PEER_TOPK - 1])
        cands = [a[i] + b[jx] for i in range(PEER_TOPK) for jx in range(PEER_TOPK)
                 if (i + 1) * (jx + 1) <= PEER_TOPK]
        top = _top_sorted(cands, PEER_TOPK)
        runner_up = _max_below(cands + [a_next + b[0], a[0] + b_next], top[PEER_TOPK - 1])
        tau = 0.5 * (top[PEER_TOPK - 1] + runner_up)
        z = jnp.ones_like(tau)
        for kk in range(1, PEER_TOPK):
            z = z + jnp.exp(top[kk] - top[0])
        inv_z = 1.0 / z
        for k in range(PEER_NKEYS):
            s1k = s1_scr[lb, k * 8:(k + 1) * 8, :]
            thr_ref[lb, k * 8:(k + 1) * 8, :] = tau - s1k
            c_ref[lb, k * 8:(k + 1) * 8, :] = jnp.exp(s1k - a[0]) * inv_z
        b0_scr[lb] = b[0]
        return carry

    lax.fori_loop(0, nlb, lane_block, 0)

    for lb in range(nlb):
        for hh in range(PEER_HEADS):
            rows = slice(hh * PEER_NKEYS, (hh + 1) * PEER_NKEYS)
            e2_ref[lb, rows, :] = jnp.exp(s2_ref[lb, rows, :] - b0_scr[lb, hh:hh + 1, :])


def _peer_prep(xn, wqt, k1kh, k2kh, k2hk):
    T, D = xn.shape
    nrow = PEER_HEADS * PEER_NKEYS
    nlb = PREP_TT // BLK
    out_sds = jax.ShapeDtypeStruct((T // BLK, nrow, BLK), F32)
    out_spec = pl.BlockSpec((nlb, nrow, BLK), lambda i: (i, 0, 0))
    return pl.pallas_call(
        _prep_kernel,
        grid=(T // PREP_TT,),
        in_specs=[pl.BlockSpec((PREP_TT, D), lambda i: (i, 0)),
                  _const_spec(wqt.shape), _const_spec(k1kh.shape), _const_spec(k2kh.shape),
                  _const_spec(k2hk.shape)],
        out_specs=[out_spec] * 4,
        out_shape=[out_sds] * 4,
        scratch_shapes=[pltpu.VMEM((nlb, nrow, BLK), F32), pltpu.VMEM((nlb, nrow, BLK), F32),
                        pltpu.VMEM((nlb, PEER_HEADS, BLK), F32)],
        compiler_params=pltpu.CompilerParams(
            dimension_semantics=("arbitrary",), vmem_limit_bytes=VMEM_LIMIT),
        name="peer_prep",
    )(xn, wqt, k1kh, k2kh, k2hk)


def _peer_kernel(xn_ref, thr_ref, c_ref, s2_ref, e2_ref, u_ref, vt_ref, x1_ref, fw_ref, out_ref,
                 yt_scr, at_scr):
    c = pl.program_id(1)
    n_i1 = PEER_EC // PEER_NKEYS
    nlb = PEER_TT // BLK

    @pl.when(c == 0)
    def _():
        yt_scr[...] = jnp.zeros_like(yt_scr)

    ht = lax.dot_general(u_ref[...], xn_ref[...], (((1,), (1,)), ((), ())), preferred_element_type=F32)
    act = jax.nn.gelu(ht)
    for ii in range(n_i1):
        row0 = pl.multiple_of((c * n_i1 + ii) * PEER_HEADS, PEER_HEADS)
        for lb in range(nlb):
            thr8 = thr_ref[lb, pl.ds(row0, PEER_HEADS), :]
            c8 = c_ref[lb, pl.ds(row0, PEER_HEADS), :]
            gate = jnp.zeros((PEER_NKEYS, BLK), F32)
            for hh in range(PEER_HEADS):
                rows = slice(hh * PEER_NKEYS, (hh + 1) * PEER_NKEYS)
                sel = jnp.where(s2_ref[lb, rows, :] >= thr8[hh:hh + 1, :], e2_ref[lb, rows, :], 0.0)
                gate = gate + sel * c8[hh:hh + 1, :]
            blk = act[ii * PEER_NKEYS:(ii + 1) * PEER_NKEYS, lb * BLK:(lb + 1) * BLK] * gate
            at_scr[ii * PEER_NKEYS:(ii + 1) * PEER_NKEYS, lb * BLK:(lb + 1) * BLK] = blk.astype(BF16)
    yt_scr[...] += jnp.dot(vt_ref[...], at_scr[...], preferred_element_type=F32)

    @pl.when(c == pl.num_programs(1) - 1)
    def _():
        xo = x1_ref[...] + yt_scr[...].T
        out_ref[...] = _rms(xo, fw_ref[...])


def _peer(xn, thr, cc, s2, e2, u, vt, x1, fw):
    T, D = xn.shape
    nrow = PEER_HEADS * PEER_NKEYS
    nlb = PEER_TT // BLK
    prep_spec = pl.BlockSpec((nlb, nrow, BLK), lambda i, c: (i, 0, 0))
    tok_spec = pl.BlockSpec((PEER_TT, D), lambda i, c: (i, 0))
    return pl.pallas_call(
        _peer_kernel,
        grid=(T // PEER_TT, PEER_N_EXPERTS // PEER_EC),
        in_specs=[tok_spec, prep_spec, prep_spec, prep_spec, prep_spec,
                  pl.BlockSpec((PEER_EC, D), lambda i, c: (c, 0)),
                  pl.BlockSpec((D, PEER_EC), lambda i, c: (0, c)),
                  tok_spec, _const_spec(fw.shape)],
        out_specs=tok_spec,
        out_shape=jax.ShapeDtypeStruct((T, D), F32),
        scratch_shapes=[pltpu.VMEM((D, PEER_TT), F32), pltpu.VMEM((PEER_EC, PEER_TT), BF16)],
        compiler_params=pltpu.CompilerParams(
            dimension_semantics=("arbitrary", "arbitrary"), vmem_limit_bytes=VMEM_LIMIT),
        name="peer_ffn",
    )(xn, thr, cc, s2, e2, u, vt, x1, fw)


def _t5_bucket(dist):
    n = np.maximum(dist, 0)
    max_exact = N_BUCKETS // 2
    large = max_exact + (np.log(np.maximum(n, 1) / max_exact) / np.log(MAX_DISTANCE / max_exact)
                         * (N_BUCKETS - max_exact)).astype(np.int32)
    large = np.minimum(large, N_BUCKETS - 1)
    return np.where(n < max_exact, n, large).astype(np.int32)


_PAIR_HEADS = [h for jj in range(Q_PER_KV) for h in (jj, Q_PER_KV + jj)]
_PAIR_COLS = np.concatenate([np.arange(h * HEAD_DIM, (h + 1) * HEAD_DIM) for h in _PAIR_HEADS])


def _band_bias(rel_bias):
    qi = np.arange(BLK)[:, None]
    kj = np.arange(2 * BLK)[None, :]
    dist = qi + BLK - kj
    in_window = (dist >= 0) & (dist < WINDOW)
    bias = rel_bias.astype(F32)[_t5_bucket(dist)]
    bias = jnp.where(in_window[:, :, None], bias, NEG_INF)
    bias = bias.transpose(2, 0, 1)[np.array(_PAIR_HEADS)]
    return bias.reshape(N_Q_HEADS * BLK, 2 * BLK)


def kernel(x, mix_norm_w, w_in, rel_bias, attn_sinks, w_branch_attn, gmlp_ln_w, gmlp_ln_b, gmlp_w_s,
           gmlp_b_s, w_branch_gmlp, w_out, ffn_norm_w, peer_w_q, peer_keys_1, peer_keys_2, peer_u,
           peer_v, final_norm_w):
    B, S, D = x.shape
    assert w_in.shape[0] == 1, "single-layer block"
    l = 0
    bias = _band_bias(rel_bias)
    causal = np.tril(np.ones((BLK, BLK), dtype=bool))
    eye = jnp.eye(PEER_HEADS, dtype=F32)
    nrow = PEER_HEADS * PEER_NKEYS

    win = jnp.concatenate([w_in[l][:, _PAIR_COLS], w_in[l][:, ATTN_WIDTH:]], axis=1).astype(BF16)
    wba = w_branch_attn[l][_PAIR_COLS, :].astype(BF16)
    sinks = attn_sinks[l].astype(F32)[np.array(_PAIR_HEADS)]
    ws = jnp.where(causal[None], gmlp_w_s[l], 0).astype(BF16)
    btab = jnp.repeat(gmlp_b_s[l].astype(F32).T, GMLP_WIDTH // GMLP_GROUPS, axis=1)
    x1, xn = _mixer(x, sinks, mix_norm_w[l].reshape(1, D), win, bias, wba,
                    gmlp_ln_w[l].reshape(1, -1), gmlp_ln_b[l].reshape(1, -1), ws, btab,
                    w_branch_gmlp[l].astype(BF16), w_out[l].astype(BF16), ffn_norm_w[l].reshape(1, D))

    wqt = peer_w_q[l].reshape(D, PEER_HEADS, 2, PEER_HALF).transpose(2, 1, 3, 0).reshape(2 * nrow, D)
    k1kh = jnp.einsum('kd,hg->khgd', peer_keys_1[l], eye).reshape(nrow, nrow).astype(BF16)
    k2kh = jnp.einsum('kd,hg->khgd', peer_keys_2[l], eye).reshape(nrow, nrow).astype(BF16)
    k2hk = jnp.einsum('kd,hg->hkgd', peer_keys_2[l], eye).reshape(nrow, nrow).astype(BF16)
    xn2 = xn.reshape(B * S, D)
    thr, cc, s2, e2 = _peer_prep(xn2, wqt.astype(BF16), k1kh, k2kh, k2hk)
    y = _peer(xn2, thr, cc, s2, e2, peer_u[l].astype(BF16), peer_v[l].T.astype(BF16),
              x1.reshape(B * S, D), final_norm_w.reshape(1, D))
    return y.reshape(B, S, D)
```

```python
import functools

import numpy as np
import jax
import jax.numpy as jnp
from jax import lax
from jax.experimental import pallas as pl
from jax.experimental.pallas import tpu as pltpu

F32 = jnp.float32
BF16 = jnp.bfloat16

D_MODEL = 1024
HEAD_DIM = 64
N_Q_HEADS = 8
N_KV_HEADS = 2
Q_PER_KV = N_Q_HEADS // N_KV_HEADS
ATTN_WIDTH = N_Q_HEADS * HEAD_DIM
KV_WIDTH = N_KV_HEADS * HEAD_DIM
WINDOW = 128
BLK = 128
N_BUCKETS = 32
MAX_DISTANCE = 128
GMLP_GROUPS = 8
GMLP_WIDTH = 512
PEER_HEADS = 8
PEER_NKEYS = 128
PEER_N_EXPERTS = PEER_NKEYS * PEER_NKEYS
PEER_HALF = 128
PEER_TOPK = 16
EPS = 1e-6
NEG_INF = -1e30

_C_Q, _C_K, _C_V, _C_GU, _C_GV, _C_GA, _C_GB, _C_END = 0, 512, 640, 768, 1280, 1792, 2816, 3840

MIX_TM = 512
PREP_TT = 512
PEER_TT = 512
PEER_EC = 512
GATE_ROWS = 32
VMEM_LIMIT = 56 * 1024 * 1024


def _rms(x, w):
    return x * lax.rsqrt(jnp.mean(x * x, axis=-1, keepdims=True) + EPS) * w


def _mixer_kernel(sink_ref, x_ref, nw_ref, win_ref, bias_ref, wba_ref, lnw_ref, lnb_ref, ws_ref,
                  btab_ref, wbg_ref, wout_ref, fnw_ref, x1_ref, xn_ref,
                  q_scr, kext, vext, gu_scr, gv_scr, ya_scr, yb_scr):
    j = pl.program_id(1)
    nb = MIX_TM // BLK
    x = x_ref[0]
    h = _rms(x, nw_ref[...]).astype(BF16)

    def proj(lo, hi):
        return jnp.dot(h, win_ref[:, lo:hi], preferred_element_type=F32)

    q_scr[...] = proj(_C_Q, _C_K) * (HEAD_DIM ** -0.5)

    @pl.when(j == 0)
    def _():
        kext[0:BLK, :] = jnp.zeros((BLK, KV_WIDTH), BF16)
        vext[0:BLK, :] = jnp.zeros((BLK, KV_WIDTH), BF16)

    kext[BLK:, :] = proj(_C_K, _C_V).astype(BF16)
    vext[BLK:, :] = proj(_C_V, _C_GU).astype(BF16)
    gu_scr[...] = jax.nn.gelu(proj(_C_GU, _C_GV))
    gv = jax.nn.gelu(proj(_C_GV, _C_GA))
    mu = jnp.mean(gv, axis=-1, keepdims=True)
    gc = gv - mu
    gvn = gc * lax.rsqrt(jnp.mean(gc * gc, axis=-1, keepdims=True) + EPS)
    gv_scr[...] = (gvn * lnw_ref[...] + lnb_ref[...]).astype(BF16)

    lane = lax.broadcasted_iota(jnp.int32, (BLK, BLK), 1)
    low = lane < HEAD_DIM
    col = lax.broadcasted_iota(jnp.int32, (BLK, 2 * BLK), 1)

    def block(b, carry):
        r0 = pl.multiple_of(b * BLK, BLK)
        qb = q_scr[pl.ds(r0, BLK), :]
        kb = kext[pl.ds(r0, 2 * BLK), :]
        vb = vext[pl.ds(r0, 2 * BLK), :]
        parts = []
        for jj in range(4):
            p_j = qb[:, jj * BLK:(jj + 1) * BLK]
            parts.append(jnp.where(low, p_j, 0.0).astype(BF16))
            parts.append(jnp.where(low, 0.0, p_j).astype(BF16))
        qs = jnp.concatenate(parts, axis=0)
        logits = lax.dot_general(qs, kb, (((1,), (1,)), ((), ())), preferred_element_type=F32)
        left_cut = jnp.where(j * nb + b == 0, BLK, 0)
        outs = []
        for r in range(8):
            lg = logits[r * BLK:(r + 1) * BLK, :] + bias_ref[r * BLK:(r + 1) * BLK, :]
            lg = jnp.where(col < left_cut, NEG_INF, lg)
            s = sink_ref[r]
            m = jnp.maximum(jnp.max(lg, axis=-1, keepdims=True), s)
            p = jnp.exp(lg - m)
            den = jnp.sum(p, axis=-1, keepdims=True) + jnp.exp(s - m)
            pv = jnp.dot(p.astype(BF16), vb, preferred_element_type=F32)
            outs.append(pv / den)
        pairs = [jnp.where(low, outs[2 * jj], outs[2 * jj + 1]) for jj in range(4)]
        ya_scr[pl.ds(r0, BLK), :] = jnp.concatenate(pairs, axis=1).astype(BF16)

        gub = gu_scr[pl.ds(r0, BLK), :]
        vnb = gv_scr[pl.ds(r0, BLK), :]
        ys = []
        for mm in range(4):
            vblk = vnb[:, mm * BLK:(mm + 1) * BLK]
            s0 = jnp.dot(ws_ref[2 * mm], vblk, preferred_element_type=F32)
            s1 = jnp.dot(ws_ref[2 * mm + 1], vblk, preferred_element_type=F32)
            sg = jnp.where(low, s0, s1) + btab_ref[:, mm * BLK:(mm + 1) * BLK]
            ys.append(gub[:, mm * BLK:(mm + 1) * BLK] * sg)
        yb_scr[pl.ds(r0, BLK), :] = jnp.concatenate(ys, axis=1).astype(BF16)
        return carry

    lax.fori_loop(0, nb, block, 0)

    kext[0:BLK, :] = kext[MIX_TM:MIX_TM + BLK, :]
    vext[0:BLK, :] = vext[MIX_TM:MIX_TM + BLK, :]

    ya_p = jnp.dot(ya_scr[...], wba_ref[...], preferred_element_type=F32)
    yb_p = jnp.dot(yb_scr[...], wbg_ref[...], preferred_element_type=F32)
    merged = jax.nn.sigmoid(proj(_C_GA, _C_GB)) * ya_p + jax.nn.sigmoid(proj(_C_GB, _C_END)) * yb_p
    x1 = x + jnp.dot(merged.astype(BF16), wout_ref[...], preferred_element_type=F32)
    x1_ref[0] = x1
    xn_ref[0] = _rms(x1, fnw_ref[...]).astype(BF16)


def _const_spec(shape):
    nd = len(shape)
    return pl.BlockSpec(shape, lambda *_: (0,) * nd)


def _mixer(x, sinks, nw, win, bias, wba, lnw, lnb, ws, btab, wbg, wout, fnw):
    B, S, D = x.shape
    grid = (B, S // MIX_TM)
    tile = pl.BlockSpec((1, MIX_TM, D), lambda b, j: (b, j, 0))
    in_specs = [
        pl.BlockSpec(memory_space=pltpu.SMEM),
        tile,
        _const_spec(nw.shape), _const_spec(win.shape), _const_spec(bias.shape), _const_spec(wba.shape),
        _const_spec(lnw.shape), _const_spec(lnb.shape), _const_spec(ws.shape), _const_spec(btab.shape),
        _const_spec(wbg.shape), _const_spec(wout.shape), _const_spec(fnw.shape),
    ]
    return pl.pallas_call(
        _mixer_kernel,
        grid=grid,
        in_specs=in_specs,
        out_specs=[tile, tile],
        out_shape=[jax.ShapeDtypeStruct((B, S, D), F32), jax.ShapeDtypeStruct((B, S, D), BF16)],
        scratch_shapes=[
            pltpu.VMEM((MIX_TM, ATTN_WIDTH), F32),
            pltpu.VMEM((MIX_TM + BLK, KV_WIDTH), BF16),
            pltpu.VMEM((MIX_TM + BLK, KV_WIDTH), BF16),
            pltpu.VMEM((MIX_TM, GMLP_WIDTH), F32),
            pltpu.VMEM((MIX_TM, GMLP_WIDTH), BF16),
            pltpu.VMEM((MIX_TM, ATTN_WIDTH), BF16),
            pltpu.VMEM((MIX_TM, GMLP_WIDTH), BF16),
        ],
        compiler_params=pltpu.CompilerParams(
            dimension_semantics=("arbitrary", "arbitrary"), vmem_limit_bytes=VMEM_LIMIT),
        name="mixer",
    )(sinks, x, nw, win, bias, wba, lnw, lnb, ws, btab, wbg, wout, fnw)


def _cmpx(a, b):
    if a is None:
        return b, None
    if b is None:
        return a, None
    return jnp.maximum(a, b), jnp.minimum(a, b)


def _bitonic_sort_desc(v):
    v = list(v)
    n = len(v)
    k = 2
    while k <= n:
        jj = k // 2
        while jj >= 1:
            for i in range(n):
                l = i ^ jj
                if l > i:
                    hi, lo = _cmpx(v[i], v[l])
                    if (i & k) == 0:
                        v[i], v[l] = hi, lo
                    else:
                        v[i], v[l] = lo, hi
            jj //= 2
        k *= 2
    return v


def _merge_top(a, b):
    n = len(a)
    c = [_cmpx(a[i], b[n - 1 - i])[0] for i in range(n)]
    jj = n // 2
    while jj >= 1:
        for i in range(n):
            if (i & jj) == 0:
                c[i], c[i + jj] = _cmpx(c[i], c[i + jj])
        jj //= 2
    return c


def _top_sorted(vals, k):
    vals = list(vals)
    while len(vals) % k:
        vals.append(None)
    groups = [_bitonic_sort_desc(vals[g:g + k]) for g in range(0, len(vals), k)]
    while len(groups) > 1:
        nxt = [_merge_top(groups[g], groups[g + 1]) for g in range(0, len(groups) - 1, 2)]
        if len(groups) % 2:
            nxt.append(groups[-1])
        groups = nxt
    return groups[0]


def _max_below(vals, bound):
    m = None
    for v in vals:
        w = jnp.where(v < bound, v, -3.0e38)
        m = w if m is None else jnp.maximum(m, w)
    return m


def _prep_kernel(xn_ref, wqt_ref, k1_ref, k2kh_ref, k2hk_ref, thr_ref, c_ref, s2_ref, e2_ref,
                 s1_scr, s2kh_scr, b0_scr):
    nlb = PREP_TT // BLK
    xn = xn_ref[...]
    qt = lax.dot_general(wqt_ref[...], xn, (((1,), (1,)), ((), ())), preferred_element_type=F32)
    nrow = PEER_HEADS * PEER_HALF
    q1 = qt[:nrow].astype(BF16)
    q2 = qt[nrow:].astype(BF16)
    s1 = jnp.dot(k1_ref[...], q1, preferred_element_type=F32)
    s2kh = jnp.dot(k2kh_ref[...], q2, preferred_element_type=F32)
    s2hk = jnp.dot(k2hk_ref[...], q2, preferred_element_type=F32)
    for lb in range(nlb):
        s1_scr[lb] = s1[:, lb * BLK:(lb + 1) * BLK]
        s2kh_scr[lb] = s2kh[:, lb * BLK:(lb + 1) * BLK]
        s2_ref[lb] = s2hk[:, lb * BLK:(lb + 1) * BLK]

    def lane_block(lb, carry):
        s1_rows = [s1_scr[lb, k * 8:(k + 1) * 8, :] for k in range(PEER_NKEYS)]
        s2_rows = [s2kh_scr[lb, k * 8:(k + 1) * 8, :] for k in range(PEER_NKEYS)]
        a = _top_sorted(s1_rows, PEER_TOPK)
        b = _top_sorted(s2_rows, PEER_TOPK)
        a_next = _max_below(s1_rows, a[PEER_TOPK - 1])
        b_next = _max_below(s2_rows, b[PEER_TOPK - 1])
        cands = [a[i] + b[jx] for i in range(PEER_TOPK) for jx in range(PEER_TOPK)
                 if (i + 1) * (jx + 1) <= PEER_TOPK]
        top = _top_sorted(cands, PEER_TOPK)
        runner_up = _max_below(cands + [a_next + b[0], a[0] + b_next], top[PEER_TOPK - 1])
        tau = 0.5 * (top[PEER_TOPK - 1] + runner_up)
        z = jnp.ones_like(tau)
        for kk in range(1, PEER_TOPK):
            z = z + jnp.exp(top[kk] - top[0])
        inv_z = 1.0 / z
        for k in range(PEER_NKEYS):
            s1k = s1_scr[lb, k * 8:(k + 1) * 8, :]
            thr_ref[lb, k * 8:(k + 1) * 8, :] = tau - s1k
            c_ref[lb, k * 8:(k + 1) * 8, :] = jnp.exp(s1k - a[0]) * inv_z
        b0_scr[lb] = b[0]
        return carry

    lax.fori_loop(0, nlb, lane_block, 0)

    for lb in range(nlb):
        for hh in range(PEER_HEADS):
            rows = slice(hh * PEER_NKEYS, (hh + 1) * PEER_NKEYS)
            e2_ref[lb, rows, :] = jnp.exp(s2_ref[lb, rows, :] - b0_scr[lb, hh:hh + 1, :])


def _peer_prep(xn, wqt, k1kh, k2kh, k2hk):
    T, D = xn.shape
    nrow = PEER_HEADS * PEER_NKEYS
    nlb = PREP_TT // BLK
    out_sds = jax.ShapeDtypeStruct((T // BLK, nrow, BLK), F32)
    out_spec = pl.BlockSpec((nlb, nrow, BLK), lambda i: (i, 0, 0))
    return pl.pallas_call(
        _prep_kernel,
        grid=(T // PREP_TT,),
        in_specs=[pl.BlockSpec((PREP_TT, D), lambda i: (i, 0)),
                  _const_spec(wqt.shape), _const_spec(k1kh.shape), _const_spec(k2kh.shape),
                  _const_spec(k2hk.shape)],
        out_specs=[out_spec] * 4,
        out_shape=[out_sds] * 4,
        scratch_shapes=[pltpu.VMEM((nlb, nrow, BLK), F32), pltpu.VMEM((nlb, nrow, BLK), F32),
                        pltpu.VMEM((nlb, PEER_HEADS, BLK), F32)],
        compiler_params=pltpu.CompilerParams(
            dimension_semantics=("arbitrary",), vmem_limit_bytes=VMEM_LIMIT),
        name="peer_prep",
    )(xn, wqt, k1kh, k2kh, k2hk)


def _peer_stage(c_gate, xn_ref, thr_ref, c_ref, s2_ref, e2_ref, u_ref, vt_ref, yt_scr,
                ht_w, ht_r, at_w, at_r):
    n_i1 = PEER_EC // PEER_NKEYS
    nlb = PEER_TT // BLK
    yt_scr[...] += jnp.dot(vt_ref[...], at_r[...], preferred_element_type=F32)
    ht_w[...] = lax.dot_general(u_ref[...], xn_ref[...], (((1,), (1,)), ((), ())),
                                preferred_element_type=F32)
    row0 = c_gate * (n_i1 * PEER_HEADS)
    for lb in range(nlb):
        lanes = slice(lb * BLK, (lb + 1) * BLK)
        for rb in range(PEER_NKEYS // GATE_ROWS):
            gates = [jnp.zeros((GATE_ROWS, BLK), F32) for _ in range(n_i1)]
            for hh in range(PEER_HEADS):
                rows = slice(hh * PEER_NKEYS + rb * GATE_ROWS, hh * PEER_NKEYS + (rb + 1) * GATE_ROWS)
                s2 = s2_ref[lb, rows, :]
                e2 = e2_ref[lb, rows, :]
                for ii in range(n_i1):
                    r = row0 + ii * PEER_HEADS + hh
                    thr = thr_ref[lb, pl.ds(r, 1), :]
                    cc = c_ref[lb, pl.ds(r, 1), :]
                    gates[ii] = gates[ii] + jnp.where(s2 >= thr, e2, 0.0) * cc
            for ii in range(n_i1):
                rows = slice(ii * PEER_NKEYS + rb * GATE_ROWS, ii * PEER_NKEYS + (rb + 1) * GATE_ROWS)
                at_w[rows, lanes] = (jax.nn.gelu(ht_r[rows, lanes]) * gates[ii]).astype(BF16)


def _peer_kernel(xn_ref, thr_ref, c_ref, s2_ref, e2_ref, u_ref, vt_ref, x1_ref, fw_ref, out_ref,
                 yt_scr, ht_a, ht_b, at_a, at_b):
    s = pl.program_id(1)
    n_chunks = PEER_N_EXPERTS // PEER_EC

    @pl.when(s == 0)
    def _():
        yt_scr[...] = jnp.zeros_like(yt_scr)
        ht_a[...] = jnp.zeros_like(ht_a)
        ht_b[...] = jnp.zeros_like(ht_b)
        at_a[...] = jnp.zeros_like(at_a)
        at_b[...] = jnp.zeros_like(at_b)

    c_gate = jnp.clip(s - 1, 0, n_chunks - 1)
    stage = functools.partial(_peer_stage, c_gate, xn_ref, thr_ref, c_ref, s2_ref, e2_ref, u_ref,
                              vt_ref, yt_scr)
    even = lax.rem(s, 2) == 0

    @pl.when(even)
    def _():
        stage(ht_a, ht_b, at_b, at_a)

    @pl.when(jnp.logical_not(even))
    def _():
        stage(ht_b, ht_a, at_a, at_b)

    @pl.when(s == pl.num_programs(1) - 1)
    def _():
        xo = x1_ref[...] + yt_scr[...].T
        out_ref[...] = _rms(xo, fw_ref[...])


def _peer(xn, thr, cc, s2, e2, u, vt, x1, fw):
    T, D = xn.shape
    nrow = PEER_HEADS * PEER_NKEYS
    nlb = PEER_TT // BLK
    n_chunks = PEER_N_EXPERTS // PEER_EC
    prep_spec = pl.BlockSpec((nlb, nrow, BLK), lambda i, s: (i, 0, 0))
    tok_spec = pl.BlockSpec((PEER_TT, D), lambda i, s: (i, 0))
    return pl.pallas_call(
        _peer_kernel,
        grid=(T // PEER_TT, n_chunks + 2),
        in_specs=[tok_spec, prep_spec, prep_spec, prep_spec, prep_spec,
                  pl.BlockSpec((PEER_EC, D), lambda i, s: (jnp.minimum(s, n_chunks - 1), 0)),
                  pl.BlockSpec((D, PEER_EC), lambda i, s: (0, jnp.clip(s - 2, 0, n_chunks - 1))),
                  tok_spec, _const_spec(fw.shape)],
        out_specs=tok_spec,
        out_shape=jax.ShapeDtypeStruct((T, D), F32),
        scratch_shapes=[pltpu.VMEM((D, PEER_TT), F32),
                        pltpu.VMEM((PEER_EC, PEER_TT), F32), pltpu.VMEM((PEER_EC, PEER_TT), F32),
                        pltpu.VMEM((PEER_EC, PEER_TT), BF16), pltpu.VMEM((PEER_EC, PEER_TT), BF16)],
        compiler_params=pltpu.CompilerParams(
            dimension_semantics=("arbitrary", "arbitrary"), vmem_limit_bytes=VMEM_LIMIT),
        name="peer_ffn",
    )(xn, thr, cc, s2, e2, u, vt, x1, fw)


def _t5_bucket(dist):
    n = np.maximum(dist, 0)
    max_exact = N_BUCKETS // 2
    large = max_exact + (np.log(np.maximum(n, 1) / max_exact) / np.log(MAX_DISTANCE / max_exact)
                         * (N_BUCKETS - max_exact)).astype(np.int32)
    large = np.minimum(large, N_BUCKETS - 1)
    return np.where(n < max_exact, n, large).astype(np.int32)


_PAIR_HEADS = [h for jj in range(Q_PER_KV) for h in (jj, Q_PER_KV + jj)]
_PAIR_COLS = np.concatenate([np.arange(h * HEAD_DIM, (h + 1) * HEAD_DIM) for h in _PAIR_HEADS])


def _band_bias(rel_bias):
    qi = np.arange(BLK)[:, None]
    kj = np.arange(2 * BLK)[None, :]
    dist = qi + BLK - kj
    in_window = (dist >= 0) & (dist < WINDOW)
    bias = rel_bias.astype(F32)[_t5_bucket(dist)]
    bias = jnp.where(in_window[:, :, None], bias, NEG_INF)
    bias = bias.transpose(2, 0, 1)[np.array(_PAIR_HEADS)]
    return bias.reshape(N_Q_HEADS * BLK, 2 * BLK)


def kernel(x, mix_norm_w, w_in, rel_bias, attn_sinks, w_branch_attn, gmlp_ln_w, gmlp_ln_b, gmlp_w_s,
           gmlp_b_s, w_branch_gmlp, w_out, ffn_norm_w, peer_w_q, peer_keys_1, peer_keys_2, peer_u,
           peer_v, final_norm_w):
    B, S, D = x.shape
    assert w_in.shape[0] == 1, "single-layer block"
    l = 0
    bias = _band_bias(rel_bias)
    causal = np.tril(np.ones((BLK, BLK), dtype=bool))
    eye = jnp.eye(PEER_HEADS, dtype=F32)
    nrow = PEER_HEADS * PEER_NKEYS

    win = jnp.concatenate([w_in[l][:, _PAIR_COLS], w_in[l][:, ATTN_WIDTH:]], axis=1).astype(BF16)
    wba = w_branch_attn[l][_PAIR_COLS, :].astype(BF16)
    sinks = attn_sinks[l].astype(F32)[np.array(_PAIR_HEADS)]
    ws = jnp.where(causal[None], gmlp_w_s[l], 0).astype(BF16)
    btab = jnp.repeat(gmlp_b_s[l].astype(F32).T, GMLP_WIDTH // GMLP_GROUPS, axis=1)
    x1, xn = _mixer(x, sinks, mix_norm_w[l].reshape(1, D), win, bias, wba,
                    gmlp_ln_w[l].reshape(1, -1), gmlp_ln_b[l].reshape(1, -1), ws, btab,
                    w_branch_gmlp[l].astype(BF16), w_out[l].astype(BF16), ffn_norm_w[l].reshape(1, D))

    wqt = peer_w_q[l].reshape(D, PEER_HEADS, 2, PEER_HALF).transpose(2, 1, 3, 0).reshape(2 * nrow, D)
    k1kh = jnp.einsum('kd,hg->khgd', peer_keys_1[l], eye).reshape(nrow, nrow).astype(BF16)
    k2kh = jnp.einsum('kd,hg->khgd', peer_keys_2[l], eye).reshape(nrow, nrow).astype(BF16)
    k2hk = jnp.einsum('kd,hg->hkgd', peer_keys_2[l], eye).reshape(nrow, nrow).astype(BF16)
    xn2 = xn.reshape(B * S, D)
    thr, cc, s2, e2 = _peer_prep(xn2, wqt.astype(BF16), k1kh, k2kh, k2hk)
    y = _peer(xn2, thr, cc, s2, e2, peer_u[l].astype(BF16), peer_v[l].T.astype(BF16),
              x1.reshape(B * S, D), final_norm_w.reshape(1, D))
    return y.reshape(B, S, D)
```

```python
import functools

import numpy as np
import jax
import jax.numpy as jnp
from jax import lax
from jax.experimental import pallas as pl
from jax.experimental.pallas import tpu as pltpu

F32 = jnp.float32
BF16 = jnp.bfloat16

D_MODEL = 1024
HEAD_DIM = 64
N_Q_HEADS = 8
N_KV_HEADS = 2
Q_PER_KV = N_Q_HEADS // N_KV_HEADS
ATTN_WIDTH = N_Q_HEADS * HEAD_DIM
KV_WIDTH = N_KV_HEADS * HEAD_DIM
WINDOW = 128
BLK = 128
N_BUCKETS = 32
MAX_DISTANCE = 128
GMLP_GROUPS = 8
GMLP_WIDTH = 512
PEER_HEADS = 8
PEER_NKEYS = 128
PEER_N_EXPERTS = PEER_NKEYS * PEER_NKEYS
PEER_HALF = 128
PEER_TOPK = 16
EPS = 1e-6
NEG_INF = -1e30

_C_Q, _C_K, _C_V, _C_GU, _C_GV, _C_GA, _C_GB, _C_END = 0, 512, 640, 768, 1280, 1792, 2816, 3840

MIX_TM = 512
PREP_TT = 512
PEER_TT = 512
PEER_EC = 512
GATE_ROWS = 64
VMEM_LIMIT = 56 * 1024 * 1024


def _rms(x, w):
    return x * lax.rsqrt(jnp.mean(x * x, axis=-1, keepdims=True) + EPS) * w


def _mixer_kernel(sink_ref, x_ref, nw_ref, win_ref, bias_ref, wba_ref, lnw_ref, lnb_ref, ws_ref,
                  btab_ref, wbg_ref, wout_ref, fnw_ref, x1_ref, xn_ref,
                  q_scr, kext, vext, gu_scr, gv_scr, ya_scr, yb_scr):
    j = pl.program_id(1)
    nb = MIX_TM // BLK
    x = x_ref[0]
    h = _rms(x, nw_ref[...]).astype(BF16)

    def proj(lo, hi):
        return jnp.dot(h, win_ref[:, lo:hi], preferred_element_type=F32)

    q_scr[...] = proj(_C_Q, _C_K) * (HEAD_DIM ** -0.5)

    @pl.when(j == 0)
    def _():
        kext[0:BLK, :] = jnp.zeros((BLK, KV_WIDTH), BF16)
        vext[0:BLK, :] = jnp.zeros((BLK, KV_WIDTH), BF16)

    kext[BLK:, :] = proj(_C_K, _C_V).astype(BF16)
    vext[BLK:, :] = proj(_C_V, _C_GU).astype(BF16)
    gu_scr[...] = jax.nn.gelu(proj(_C_GU, _C_GV))
    gv = jax.nn.gelu(proj(_C_GV, _C_GA))
    mu = jnp.mean(gv, axis=-1, keepdims=True)
    gc = gv - mu
    gvn = gc * lax.rsqrt(jnp.mean(gc * gc, axis=-1, keepdims=True) + EPS)
    gv_scr[...] = (gvn * lnw_ref[...] + lnb_ref[...]).astype(BF16)

    lane = lax.broadcasted_iota(jnp.int32, (BLK, BLK), 1)
    low = lane < HEAD_DIM
    col = lax.broadcasted_iota(jnp.int32, (BLK, 2 * BLK), 1)

    def block(b, carry):
        r0 = pl.multiple_of(b * BLK, BLK)
        qb = q_scr[pl.ds(r0, BLK), :]
        kb = kext[pl.ds(r0, 2 * BLK), :]
        vb = vext[pl.ds(r0, 2 * BLK), :]
        parts = []
        for jj in range(4):
            p_j = qb[:, jj * BLK:(jj + 1) * BLK]
            parts.append(jnp.where(low, p_j, 0.0).astype(BF16))
            parts.append(jnp.where(low, 0.0, p_j).astype(BF16))
        qs = jnp.concatenate(parts, axis=0)
        logits = lax.dot_general(qs, kb, (((1,), (1,)), ((), ())), preferred_element_type=F32)
        left_cut = jnp.where(j * nb + b == 0, BLK, 0)
        outs = []
        for r in range(8):
            lg = logits[r * BLK:(r + 1) * BLK, :] + bias_ref[r * BLK:(r + 1) * BLK, :]
            lg = jnp.where(col < left_cut, NEG_INF, lg)
            s = sink_ref[r]
            m = jnp.maximum(jnp.max(lg, axis=-1, keepdims=True), s)
            p = jnp.exp(lg - m)
            den = jnp.sum(p, axis=-1, keepdims=True) + jnp.exp(s - m)
            pv = jnp.dot(p.astype(BF16), vb, preferred_element_type=F32)
            outs.append(pv / den)
        pairs = [jnp.where(low, outs[2 * jj], outs[2 * jj + 1]) for jj in range(4)]
        ya_scr[pl.ds(r0, BLK), :] = jnp.concatenate(pairs, axis=1).astype(BF16)

        gub = gu_scr[pl.ds(r0, BLK), :]
        vnb = gv_scr[pl.ds(r0, BLK), :]
        ys = []
        for mm in range(4):
            vblk = vnb[:, mm * BLK:(mm + 1) * BLK]
            s0 = jnp.dot(ws_ref[2 * mm], vblk, preferred_element_type=F32)
            s1 = jnp.dot(ws_ref[2 * mm + 1], vblk, preferred_element_type=F32)
            sg = jnp.where(low, s0, s1) + btab_ref[:, mm * BLK:(mm + 1) * BLK]
            ys.append(gub[:, mm * BLK:(mm + 1) * BLK] * sg)
        yb_scr[pl.ds(r0, BLK), :] = jnp.concatenate(ys, axis=1).astype(BF16)
        return carry

    lax.fori_loop(0, nb, block, 0)

    kext[0:BLK, :] = kext[MIX_TM:MIX_TM + BLK, :]
    vext[0:BLK, :] = vext[MIX_TM:MIX_TM + BLK, :]

    ya_p = jnp.dot(ya_scr[...], wba_ref[...], preferred_element_type=F32)
    yb_p = jnp.dot(yb_scr[...], wbg_ref[...], preferred_element_type=F32)
    merged = jax.nn.sigmoid(proj(_C_GA, _C_GB)) * ya_p + jax.nn.sigmoid(proj(_C_GB, _C_END)) * yb_p
    x1 = x + jnp.dot(merged.astype(BF16), wout_ref[...], preferred_element_type=F32)
    x1_ref[0] = x1
    xn_ref[0] = _rms(x1, fnw_ref[...]).astype(BF16)


def _const_spec(shape):
    nd = len(shape)
    return pl.BlockSpec(shape, lambda *_: (0,) * nd)


def _mixer(x, sinks, nw, win, bias, wba, lnw, lnb, ws, btab, wbg, wout, fnw):
    B, S, D = x.shape
    grid = (B, S // MIX_TM)
    tile = pl.BlockSpec((1, MIX_TM, D), lambda b, j: (b, j, 0))
    in_specs = [
        pl.BlockSpec(memory_space=pltpu.SMEM),
        tile,
        _const_spec(nw.shape), _const_spec(win.shape), _const_spec(bias.shape), _const_spec(wba.shape),
        _const_spec(lnw.shape), _const_spec(lnb.shape), _const_spec(ws.shape), _const_spec(btab.shape),
        _const_spec(wbg.shape), _const_spec(wout.shape), _const_spec(fnw.shape),
    ]
    return pl.pallas_call(
        _mixer_kernel,
        grid=grid,
        in_specs=in_specs,
        out_specs=[tile, tile],
        out_shape=[jax.ShapeDtypeStruct((B, S, D), F32), jax.ShapeDtypeStruct((B, S, D), BF16)],
        scratch_shapes=[
            pltpu.VMEM((MIX_TM, ATTN_WIDTH), F32),
            pltpu.VMEM((MIX_TM + BLK, KV_WIDTH), BF16),
            pltpu.VMEM((MIX_TM + BLK, KV_WIDTH), BF16),
            pltpu.VMEM((MIX_TM, GMLP_WIDTH), F32),
            pltpu.VMEM((MIX_TM, GMLP_WIDTH), BF16),
            pltpu.VMEM((MIX_TM, ATTN_WIDTH), BF16),
            pltpu.VMEM((MIX_TM, GMLP_WIDTH), BF16),
        ],
        compiler_params=pltpu.CompilerParams(
            dimension_semantics=("arbitrary", "arbitrary"), vmem_limit_bytes=VMEM_LIMIT),
        name="mixer",
    )(sinks, x, nw, win, bias, wba, lnw, lnb, ws, btab, wbg, wout, fnw)


def _cmpx(a, b):
    if a is None:
        return b, None
    if b is None:
        return a, None
    return jnp.maximum(a, b), jnp.minimum(a, b)


def _bitonic_sort_desc(v):
    v = list(v)
    n = len(v)
    k = 2
    while k <= n:
        jj = k // 2
        while jj >= 1:
            for i in range(n):
                l = i ^ jj
                if l > i:
                    hi, lo = _cmpx(v[i], v[l])
                    if (i & k) == 0:
                        v[i], v[l] = hi, lo
                    else:
                        v[i], v[l] = lo, hi
            jj //= 2
        k *= 2
    return v


def _merge_top(a, b):
    n = len(a)
    c = [_cmpx(a[i], b[n - 1 - i])[0] for i in range(n)]
    jj = n // 2
    while jj >= 1:
        for i in range(n):
            if (i & jj) == 0:
                c[i], c[i + jj] = _cmpx(c[i], c[i + jj])
        jj //= 2
    return c


def _top_sorted(vals, k):
    vals = list(vals)
    while len(vals) % k:
        vals.append(None)
    groups = [_bitonic_sort_desc(vals[g:g + k]) for g in range(0, len(vals), k)]
    while len(groups) > 1:
        nxt = [_merge_top(groups[g], groups[g + 1]) for g in range(0, len(groups) - 1, 2)]
        if len(groups) % 2:
            nxt.append(groups[-1])
        groups = nxt
    return groups[0]


def _max_below(vals, bound):
    m = None
    for v in vals:
        w = jnp.where(v < bound, v, -3.0e38)
        m = w if m is None else jnp.maximum(m, w)
    return m


def _prep_kernel(xn_ref, wqt_ref, k1_ref, k2kh_ref, k2hk_ref, thr_ref, c_ref, s2_ref, e2_ref,
                 s1_scr, s2kh_scr, s2hk_scr, b0_scr):
    nlb = PREP_TT // BLK
    xn = xn_ref[...]
    qt = lax.dot_general(wqt_ref[...], xn, (((1,), (1,)), ((), ())), preferred_element_type=F32)
    nrow = PEER_HEADS * PEER_HALF
    q1 = qt[:nrow].astype(BF16)
    q2 = qt[nrow:].astype(BF16)
    s1 = jnp.dot(k1_ref[...], q1, preferred_element_type=F32)
    s2kh = jnp.dot(k2kh_ref[...], q2, preferred_element_type=F32)
    s2hk = jnp.dot(k2hk_ref[...], q2, preferred_element_type=F32)
    for lb in range(nlb):
        s1_scr[lb] = s1[:, lb * BLK:(lb + 1) * BLK]
        s2kh_scr[lb] = s2kh[:, lb * BLK:(lb + 1) * BLK]
        s2hk_scr[lb] = s2hk[:, lb * BLK:(lb + 1) * BLK]

    def lane_block(lb, carry):
        s1_rows = [s1_scr[lb, k * 8:(k + 1) * 8, :] for k in range(PEER_NKEYS)]
        s2_rows = [s2kh_scr[lb, k * 8:(k + 1) * 8, :] for k in range(PEER_NKEYS)]
        a = _top_sorted(s1_rows, PEER_TOPK)
        b = _top_sorted(s2_rows, PEER_TOPK)
        a_next = _max_below(s1_rows, a[PEER_TOPK - 1])
        b_next = _max_below(s2_rows, b[PEER_TOPK - 1])
        cands = [a[i] + b[jx] for i in range(PEER_TOPK) for jx in range(PEER_TOPK)
                 if (i + 1) * (jx + 1) <= PEER_TOPK]
        top = _top_sorted(cands, PEER_TOPK)
        runner_up = _max_below(cands + [a_next + b[0], a[0] + b_next], top[PEER_TOPK - 1])
        tau = 0.5 * (top[PEER_TOPK - 1] + runner_up)
        z = jnp.ones_like(tau)
        for kk in range(1, PEER_TOPK):
            z = z + jnp.exp(top[kk] - top[0])
        inv_z = 1.0 / z
        tau_rel = tau - b[0]
        for k in range(PEER_NKEYS):
            s1k = s1_scr[lb, k * 8:(k + 1) * 8, :]
            thr_ref[lb, k * 8:(k + 1) * 8, :] = tau_rel - s1k
            c_ref[lb, k * 8:(k + 1) * 8, :] = jnp.exp(s1k - a[0]) * inv_z
        b0_scr[lb] = b[0]
        return carry

    lax.fori_loop(0, nlb, lane_block, 0)

    for lb in range(nlb):
        for hh in range(PEER_HEADS):
            rows = slice(hh * PEER_NKEYS, (hh + 1) * PEER_NKEYS)
            s2_rel = s2hk_scr[lb, rows, :] - b0_scr[lb, hh:hh + 1, :]
            s2_ref[lb, rows, :] = s2_rel.astype(BF16)
            e2_ref[lb, rows, :] = jnp.exp(s2_rel).astype(BF16)


def _peer_prep(xn, wqt, k1kh, k2kh, k2hk):
    T, D = xn.shape
    nrow = PEER_HEADS * PEER_NKEYS
    nlb = PREP_TT // BLK
    out_spec = pl.BlockSpec((nlb, nrow, BLK), lambda i: (i, 0, 0))
    rows_f32 = jax.ShapeDtypeStruct((T // BLK, nrow, BLK), F32)
    tiles_bf16 = jax.ShapeDtypeStruct((T // BLK, nrow, BLK), BF16)
    return pl.pallas_call(
        _prep_kernel,
        grid=(T // PREP_TT,),
        in_specs=[pl.BlockSpec((PREP_TT, D), lambda i: (i, 0)),
                  _const_spec(wqt.shape), _const_spec(k1kh.shape), _const_spec(k2kh.shape),
                  _const_spec(k2hk.shape)],
        out_specs=[out_spec] * 4,
        out_shape=[rows_f32, rows_f32, tiles_bf16, tiles_bf16],
        scratch_shapes=[pltpu.VMEM((nlb, nrow, BLK), F32), pltpu.VMEM((nlb, nrow, BLK), F32),
                        pltpu.VMEM((nlb, nrow, BLK), F32), pltpu.VMEM((nlb, PEER_HEADS, BLK), F32)],
        compiler_params=pltpu.CompilerParams(
            dimension_semantics=("arbitrary",), vmem_limit_bytes=VMEM_LIMIT),
        name="peer_prep",
    )(xn, wqt, k1kh, k2kh, k2hk)


def _bf16_rows(row):
    return jnp.broadcast_to(row, (GATE_ROWS, BLK)).astype(BF16)


def _peer_stage(c_gate, xn_ref, thr_ref, c_ref, s2_ref, e2_ref, u_ref, vt_ref, yt_scr,
                ht_w, ht_r, at_w, at_r):
    n_i1 = PEER_EC // PEER_NKEYS
    nlb = PEER_TT // BLK
    yt_scr[...] += jnp.dot(vt_ref[...], at_r[...], preferred_element_type=F32)
    ht_w[...] = lax.dot_general(u_ref[...], xn_ref[...], (((1,), (1,)), ((), ())),
                                preferred_element_type=F32)
    row0 = c_gate * (n_i1 * PEER_HEADS)
    for lb in range(nlb):
        lanes = slice(lb * BLK, (lb + 1) * BLK)
        for rb in range(PEER_NKEYS // GATE_ROWS):
            gates = [jnp.zeros((GATE_ROWS, BLK), BF16) for _ in range(n_i1)]
            for hh in range(PEER_HEADS):
                rows = slice(hh * PEER_NKEYS + rb * GATE_ROWS, hh * PEER_NKEYS + (rb + 1) * GATE_ROWS)
                s2 = s2_ref[lb, rows, :]
                e2 = e2_ref[lb, rows, :]
                for ii in range(n_i1):
                    r = row0 + ii * PEER_HEADS + hh
                    thr = _bf16_rows(thr_ref[lb, pl.ds(r, 1), :])
                    cc = _bf16_rows(c_ref[lb, pl.ds(r, 1), :])
                    gates[ii] = gates[ii] + jnp.where(s2 >= thr, e2, jnp.zeros_like(e2)) * cc
            for ii in range(n_i1):
                rows = slice(ii * PEER_NKEYS + rb * GATE_ROWS, ii * PEER_NKEYS + (rb + 1) * GATE_ROWS)
                at_w[rows, lanes] = jax.nn.gelu(ht_r[rows, lanes]).astype(BF16) * gates[ii]


def _peer_kernel(xn_ref, thr_ref, c_ref, s2_in, e2_in, u_ref, vt_ref, x1_ref, fw_ref, out_ref,
                 yt_scr, ht_a, ht_b, at_a, at_b, s2_ref, e2_ref):
    s = pl.program_id(1)
    n_chunks = PEER_N_EXPERTS // PEER_EC

    @pl.when(s == 0)
    def _():
        yt_scr[...] = jnp.zeros_like(yt_scr)
        ht_a[...] = jnp.zeros_like(ht_a)
        ht_b[...] = jnp.zeros_like(ht_b)
        at_a[...] = jnp.zeros_like(at_a)
        at_b[...] = jnp.zeros_like(at_b)
        s2_ref[...] = s2_in[...]
        e2_ref[...] = e2_in[...]

    c_gate = jnp.clip(s - 1, 0, n_chunks - 1)
    stage = functools.partial(_peer_stage, c_gate, xn_ref, thr_ref, c_ref, s2_ref, e2_ref, u_ref,
                              vt_ref, yt_scr)
    even = lax.rem(s, 2) == 0

    @pl.when(even)
    def _():
        stage(ht_a, ht_b, at_b, at_a)

    @pl.when(jnp.logical_not(even))
    def _():
        stage(ht_b, ht_a, at_a, at_b)

    @pl.when(s == pl.num_programs(1) - 1)
    def _():
        xo = x1_ref[...] + yt_scr[...].T
        out_ref[...] = _rms(xo, fw_ref[...])


def _peer(xn, thr, cc, s2, e2, u, vt, x1, fw):
    T, D = xn.shape
    nrow = PEER_HEADS * PEER_NKEYS
    nlb = PEER_TT // BLK
    n_chunks = PEER_N_EXPERTS // PEER_EC
    prep_spec = pl.BlockSpec((nlb, nrow, BLK), lambda i, s: (i, 0, 0))
    tok_spec = pl.BlockSpec((PEER_TT, D), lambda i, s: (i, 0))
    return pl.pallas_call(
        _peer_kernel,
        grid=(T // PEER_TT, n_chunks + 2),
        in_specs=[tok_spec, prep_spec, prep_spec, prep_spec, prep_spec,
                  pl.BlockSpec((PEER_EC, D), lambda i, s: (jnp.minimum(s, n_chunks - 1), 0)),
                  pl.BlockSpec((D, PEER_EC), lambda i, s: (0, jnp.clip(s - 2, 0, n_chunks - 1))),
                  tok_spec, _const_spec(fw.shape)],
        out_specs=tok_spec,
        out_shape=jax.ShapeDtypeStruct((T, D), F32),
        scratch_shapes=[pltpu.VMEM((D, PEER_TT), F32),
                        pltpu.VMEM((PEER_EC, PEER_TT), F32), pltpu.VMEM((PEER_EC, PEER_TT), F32),
                        pltpu.VMEM((PEER_EC, PEER_TT), BF16), pltpu.VMEM((PEER_EC, PEER_TT), BF16),
                        pltpu.VMEM((nlb, nrow, BLK), BF16), pltpu.VMEM((nlb, nrow, BLK), BF16)],
        compiler_params=pltpu.CompilerParams(
            dimension_semantics=("arbitrary", "arbitrary"), vmem_limit_bytes=VMEM_LIMIT),
        name="peer_ffn",
    )(xn, thr, cc, s2, e2, u, vt, x1, fw)


def _t5_bucket(dist):
    n = np.maximum(dist, 0)
    max_exact = N_BUCKETS // 2
    large = max_exact + (np.log(np.maximum(n, 1) / max_exact) / np.log(MAX_DISTANCE / max_exact)
                         * (N_BUCKETS - max_exact)).astype(np.int32)
    large = np.minimum(large, N_BUCKETS - 1)
    return np.where(n < max_exact, n, large).astype(np.int32)


_PAIR_HEADS = [h for jj in range(Q_PER_KV) for h in (jj, Q_PER_KV + jj)]


def _band_bias(rel_bias):
    qi = np.arange(BLK)[:, None]
    kj = np.arange(2 * BLK)[None, :]
    dist = qi + BLK - kj
    in_window = (dist >= 0) & (dist < WINDOW)
    bias = rel_bias.astype(F32)[_t5_bucket(dist)]
    bias = jnp.where(in_window[:, :, None], bias, NEG_INF)
    bias = bias.transpose(2, 0, 1)[np.array(_PAIR_HEADS)]
    return bias.reshape(N_Q_HEADS * BLK, 2 * BLK)


def kernel(x, mix_norm_w, w_in, rel_bias, attn_sinks, w_branch_attn, gmlp_ln_w, gmlp_ln_b, gmlp_w_s,
           gmlp_b_s, w_branch_gmlp, w_out, ffn_norm_w, peer_w_q, peer_keys_1, peer_keys_2, peer_u,
           peer_v, final_norm_w):
    B, S, D = x.shape
    assert w_in.shape[0] == 1, "single-layer block"
    l = 0
    bias = _band_bias(rel_bias)
    causal = np.tril(np.ones((BLK, BLK), dtype=bool))
    eye = jnp.eye(PEER_HEADS, dtype=F32)
    nrow = PEER_HEADS * PEER_NKEYS

    w_q_pairs = (w_in[l][:, :ATTN_WIDTH].reshape(D, N_KV_HEADS, Q_PER_KV, HEAD_DIM)
                 .transpose(0, 2, 1, 3).reshape(D, ATTN_WIDTH))
    win = jnp.concatenate([w_q_pairs, w_in[l][:, ATTN_WIDTH:]], axis=1).astype(BF16)
    wba = (w_branch_attn[l].reshape(N_KV_HEADS, Q_PER_KV, HEAD_DIM, D)
           .transpose(1, 0, 2, 3).reshape(ATTN_WIDTH, D).astype(BF16))
    sinks = attn_sinks[l].astype(F32).reshape(N_KV_HEADS, Q_PER_KV).T.reshape(N_Q_HEADS)
    ws = jnp.where(causal[None], gmlp_w_s[l], 0).astype(BF16)
    btab = jnp.repeat(gmlp_b_s[l].astype(F32).T, GMLP_WIDTH // GMLP_GROUPS, axis=1)
    x1, xn = _mixer(x, sinks, mix_norm_w[l].reshape(1, D), win, bias, wba,
                    gmlp_ln_w[l].reshape(1, -1), gmlp_ln_b[l].reshape(1, -1), ws, btab,
                    w_branch_gmlp[l].astype(BF16), w_out[l].astype(BF16), ffn_norm_w[l].reshape(1, D))

    wqt = peer_w_q[l].reshape(D, PEER_HEADS, 2, PEER_HALF).transpose(2, 1, 3, 0).reshape(2 * nrow, D)
    k1kh = jnp.einsum('kd,hg->khgd', peer_keys_1[l], eye).reshape(nrow, nrow).astype(BF16)
    k2kh = jnp.einsum('kd,hg->khgd', peer_keys_2[l], eye).reshape(nrow, nrow).astype(BF16)
    k2hk = jnp.einsum('kd,hg->hkgd', peer_keys_2[l], eye).reshape(nrow, nrow).astype(BF16)
    xn2 = xn.reshape(B * S, D)
    thr, cc, s2, e2 = _peer_prep(xn2, wqt.astype(BF16), k1kh, k2kh, k2hk)
    y = _peer(xn2, thr, cc, s2, e2, peer_u[l].astype(BF16), peer_v[l].T.astype(BF16),
              x1.reshape(B * S, D), final_norm_w.reshape(1, D))
    return y.reshape(B, S, D)
```

```python
import functools

import numpy as np
import jax
import jax.numpy as jnp
from jax import lax
from jax.experimental import pallas as pl
from jax.experimental.pallas import tpu as pltpu

F32 = jnp.float32
BF16 = jnp.bfloat16

D_MODEL = 1024
HEAD_DIM = 64
N_Q_HEADS = 8
N_KV_HEADS = 2
Q_PER_KV = N_Q_HEADS // N_KV_HEADS
ATTN_WIDTH = N_Q_HEADS * HEAD_DIM
KV_WIDTH = N_KV_HEADS * HEAD_DIM
WINDOW = 128
BLK = 128
N_BUCKETS = 32
MAX_DISTANCE = 128
GMLP_GROUPS = 8
GMLP_WIDTH = 512
PEER_HEADS = 8
PEER_NKEYS = 128
PEER_N_EXPERTS = PEER_NKEYS * PEER_NKEYS
PEER_HALF = 128
PEER_TOPK = 16
EPS = 1e-6
NEG_INF = -1e30

_C_Q, _C_K, _C_V, _C_GU, _C_GV, _C_GA, _C_GB, _C_END = 0, 512, 640, 768, 1280, 1792, 2816, 3840

MIX_TM = 512
PREP_TT = 512
PEER_TT = 512
PEER_EC = 512
GATE_ROWS = 64
VMEM_LIMIT = 56 * 1024 * 1024


def _rms(x, w):
    return x * lax.rsqrt(jnp.mean(x * x, axis=-1, keepdims=True) + EPS) * w


def _mixer_kernel(sink_ref, x_ref, nw_ref, win_ref, bias_ref, wba_ref, lnw_ref, lnb_ref, ws_ref,
                  btab_ref, wbg_ref, wout_ref, fnw_ref, x1_ref, xn_ref,
                  q_scr, kext, vext, gu_scr, gv_scr, ya_scr, yb_scr):
    j = pl.program_id(1)
    nb = MIX_TM // BLK
    x = x_ref[0]
    h = _rms(x, nw_ref[...]).astype(BF16)

    def proj(lo, hi):
        return jnp.dot(h, win_ref[:, lo:hi], preferred_element_type=F32)

    q_scr[...] = proj(_C_Q, _C_K) * (HEAD_DIM ** -0.5)

    @pl.when(j == 0)
    def _():
        kext[0:BLK, :] = jnp.zeros((BLK, KV_WIDTH), BF16)
        vext[0:BLK, :] = jnp.zeros((BLK, KV_WIDTH), BF16)

    kext[BLK:, :] = proj(_C_K, _C_V).astype(BF16)
    vext[BLK:, :] = proj(_C_V, _C_GU).astype(BF16)
    gu_scr[...] = jax.nn.gelu(proj(_C_GU, _C_GV))
    gv = jax.nn.gelu(proj(_C_GV, _C_GA))
    mu = jnp.mean(gv, axis=-1, keepdims=True)
    gc = gv - mu
    gvn = gc * lax.rsqrt(jnp.mean(gc * gc, axis=-1, keepdims=True) + EPS)
    gv_scr[...] = (gvn * lnw_ref[...] + lnb_ref[...]).astype(BF16)

    lane = lax.broadcasted_iota(jnp.int32, (BLK, BLK), 1)
    low = lane < HEAD_DIM
    col = lax.broadcasted_iota(jnp.int32, (BLK, 2 * BLK), 1)

    def block(b, carry):
        r0 = pl.multiple_of(b * BLK, BLK)
        qb = q_scr[pl.ds(r0, BLK), :]
        kb = kext[pl.ds(r0, 2 * BLK), :]
        vb = vext[pl.ds(r0, 2 * BLK), :]
        parts = []
        for jj in range(4):
            p_j = qb[:, jj * BLK:(jj + 1) * BLK]
            parts.append(jnp.where(low, p_j, 0.0).astype(BF16))
            parts.append(jnp.where(low, 0.0, p_j).astype(BF16))
        qs = jnp.concatenate(parts, axis=0)
        logits = lax.dot_general(qs, kb, (((1,), (1,)), ((), ())), preferred_element_type=F32)
        left_cut = jnp.where(j * nb + b == 0, BLK, 0)
        outs = []
        for r in range(8):
            lg = logits[r * BLK:(r + 1) * BLK, :] + bias_ref[r * BLK:(r + 1) * BLK, :]
            lg = jnp.where(col < left_cut, NEG_INF, lg)
            s = sink_ref[r]
            m = jnp.maximum(jnp.max(lg, axis=-1, keepdims=True), s)
            p = jnp.exp(lg - m)
            den = jnp.sum(p, axis=-1, keepdims=True) + jnp.exp(s - m)
            pv = jnp.dot(p.astype(BF16), vb, preferred_element_type=F32)
            outs.append(pv / den)
        pairs = [jnp.where(low, outs[2 * jj], outs[2 * jj + 1]) for jj in range(4)]
        ya_scr[pl.ds(r0, BLK), :] = jnp.concatenate(pairs, axis=1).astype(BF16)

        gub = gu_scr[pl.ds(r0, BLK), :]
        vnb = gv_scr[pl.ds(r0, BLK), :]
        ys = []
        for mm in range(4):
            vblk = vnb[:, mm * BLK:(mm + 1) * BLK]
            s0 = jnp.dot(ws_ref[2 * mm], vblk, preferred_element_type=F32)
            s1 = jnp.dot(ws_ref[2 * mm + 1], vblk, preferred_element_type=F32)
            sg = jnp.where(low, s0, s1) + btab_ref[:, mm * BLK:(mm + 1) * BLK]
            ys.append(gub[:, mm * BLK:(mm + 1) * BLK] * sg)
        yb_scr[pl.ds(r0, BLK), :] = jnp.concatenate(ys, axis=1).astype(BF16)
        return carry

    lax.fori_loop(0, nb, block, 0)

    kext[0:BLK, :] = kext[MIX_TM:MIX_TM + BLK, :]
    vext[0:BLK, :] = vext[MIX_TM:MIX_TM + BLK, :]

    ya_p = jnp.dot(ya_scr[...], wba_ref[...], preferred_element_type=F32)
    yb_p = jnp.dot(yb_scr[...], wbg_ref[...], preferred_element_type=F32)
    merged = jax.nn.sigmoid(proj(_C_GA, _C_GB)) * ya_p + jax.nn.sigmoid(proj(_C_GB, _C_END)) * yb_p
    x1 = x + jnp.dot(merged.astype(BF16), wout_ref[...], preferred_element_type=F32)
    x1_ref[0] = x1
    xn_ref[0] = _rms(x1, fnw_ref[...]).astype(BF16)


def _const_spec(shape):
    nd = len(shape)
    return pl.BlockSpec(shape, lambda *_: (0,) * nd)


def _mixer(x, sinks, nw, win, bias, wba, lnw, lnb, ws, btab, wbg, wout, fnw):
    B, S, D = x.shape
    grid = (B, S // MIX_TM)
    tile = pl.BlockSpec((1, MIX_TM, D), lambda b, j: (b, j, 0))
    in_specs = [
        pl.BlockSpec(memory_space=pltpu.SMEM),
        tile,
        _const_spec(nw.shape), _const_spec(win.shape), _const_spec(bias.shape), _const_spec(wba.shape),
        _const_spec(lnw.shape), _const_spec(lnb.shape), _const_spec(ws.shape), _const_spec(btab.shape),
        _const_spec(wbg.shape), _const_spec(wout.shape), _const_spec(fnw.shape),
    ]
    return pl.pallas_call(
        _mixer_kernel,
        grid=grid,
        in_specs=in_specs,
        out_specs=[tile, tile],
        out_shape=[jax.ShapeDtypeStruct((B, S, D), F32), jax.ShapeDtypeStruct((B, S, D), BF16)],
        scratch_shapes=[
            pltpu.VMEM((MIX_TM, ATTN_WIDTH), F32),
            pltpu.VMEM((MIX_TM + BLK, KV_WIDTH), BF16),
            pltpu.VMEM((MIX_TM + BLK, KV_WIDTH), BF16),
            pltpu.VMEM((MIX_TM, GMLP_WIDTH), F32),
            pltpu.VMEM((MIX_TM, GMLP_WIDTH), BF16),
            pltpu.VMEM((MIX_TM, ATTN_WIDTH), BF16),
            pltpu.VMEM((MIX_TM, GMLP_WIDTH), BF16),
        ],
        compiler_params=pltpu.CompilerParams(
            dimension_semantics=("arbitrary", "arbitrary"), vmem_limit_bytes=VMEM_LIMIT),
        name="mixer",
    )(sinks, x, nw, win, bias, wba, lnw, lnb, ws, btab, wbg, wout, fnw)


def _cmpx(a, b):
    if a is None:
        return b, None
    if b is None:
        return a, None
    return jnp.maximum(a, b), jnp.minimum(a, b)


def _bitonic_sort_desc(v):
    v = list(v)
    n = len(v)
    k = 2
    while k <= n:
        jj = k // 2
        while jj >= 1:
            for i in range(n):
                l = i ^ jj
                if l > i:
                    hi, lo = _cmpx(v[i], v[l])
                    if (i & k) == 0:
                        v[i], v[l] = hi, lo
                    else:
                        v[i], v[l] = lo, hi
            jj //= 2
        k *= 2
    return v


def _merge_top(a, b):
    n = len(a)
    c = [_cmpx(a[i], b[n - 1 - i])[0] for i in range(n)]
    jj = n // 2
    while jj >= 1:
        for i in range(n):
            if (i & jj) == 0:
                c[i], c[i + jj] = _cmpx(c[i], c[i + jj])
        jj //= 2
    return c


def _top_sorted(vals, k):
    vals = list(vals)
    while len(vals) % k:
        vals.append(None)
    groups = [_bitonic_sort_desc(vals[g:g + k]) for g in range(0, len(vals), k)]
    while len(groups) > 1:
        nxt = [_merge_top(groups[g], groups[g + 1]) for g in range(0, len(groups) - 1, 2)]
        if len(groups) % 2:
            nxt.append(groups[-1])
        groups = nxt
    return groups[0]


def _max_below(vals, bound):
    m = None
    for v in vals:
        w = jnp.where(v < bound, v, -3.0e38)
        m = w if m is None else jnp.maximum(m, w)
    return m


def _prep_kernel(xn_ref, wqt_ref, k1_ref, k2kh_ref, k2hk_ref, thr_ref, c_ref, s2_ref, e2_ref,
                 s1_scr, s2kh_scr, s2hk_scr, b0_scr):
    nlb = PREP_TT // BLK
    xn = xn_ref[...]
    qt = lax.dot_general(wqt_ref[...], xn, (((1,), (1,)), ((), ())), preferred_element_type=F32)
    nrow = PEER_HEADS * PEER_HALF
    q1 = qt[:nrow].astype(BF16)
    q2 = qt[nrow:].astype(BF16)
    s1 = jnp.dot(k1_ref[...], q1, preferred_element_type=F32)
    s2kh = jnp.dot(k2kh_ref[...], q2, preferred_element_type=F32)
    s2hk = jnp.dot(k2hk_ref[...], q2, preferred_element_type=F32)
    for lb in range(nlb):
        s1_scr[lb] = s1[:, lb * BLK:(lb + 1) * BLK]
        s2kh_scr[lb] = s2kh[:, lb * BLK:(lb + 1) * BLK]
        s2hk_scr[lb] = s2hk[:, lb * BLK:(lb + 1) * BLK]

    def lane_block(lb, carry):
        s1_rows = [s1_scr[lb, k * 8:(k + 1) * 8, :] for k in range(PEER_NKEYS)]
        s2_rows = [s2kh_scr[lb, k * 8:(k + 1) * 8, :] for k in range(PEER_NKEYS)]
        a = _top_sorted(s1_rows, PEER_TOPK)
        b = _top_sorted(s2_rows, PEER_TOPK)
        a_next = _max_below(s1_rows, a[PEER_TOPK - 1])
        b_next = _max_below(s2_rows, b[PEER_TOPK - 1])
        cands = [a[i] + b[jx] for i in range(PEER_TOPK) for jx in range(PEER_TOPK)
                 if (i + 1) * (jx + 1) <= PEER_TOPK]
        top = _top_sorted(cands, PEER_TOPK)
        runner_up = _max_below(cands + [a_next + b[0], a[0] + b_next], top[PEER_TOPK - 1])
        tau = 0.5 * (top[PEER_TOPK - 1] + runner_up)
        z = jnp.ones_like(tau)
        for kk in range(1, PEER_TOPK):
            z = z + jnp.exp(top[kk] - top[0])
        inv_z = 1.0 / z
        tau_rel = tau - b[0]
        for k in range(PEER_NKEYS):
            s1k = s1_scr[lb, k * 8:(k + 1) * 8, :]
            thr_ref[lb, k * 8:(k + 1) * 8, :] = tau_rel - s1k
            c_ref[lb, k * 8:(k + 1) * 8, :] = jnp.exp(s1k - a[0]) * inv_z
        b0_scr[lb] = b[0]
        return carry

    lax.fori_loop(0, nlb, lane_block, 0)

    for lb in range(nlb):
        for hh in range(PEER_HEADS):
            rows = slice(hh * PEER_NKEYS, (hh + 1) * PEER_NKEYS)
            s2_rel = s2hk_scr[lb, rows, :] - b0_scr[lb, hh:hh + 1, :]
            s2_ref[lb, rows, :] = s2_rel.astype(BF16)
            e2_ref[lb, rows, :] = jnp.exp(s2_rel).astype(BF16)


def _peer_prep(xn, wqt, k1kh, k2kh, k2hk):
    T, D = xn.shape
    nrow = PEER_HEADS * PEER_NKEYS
    nlb = PREP_TT // BLK
    out_spec = pl.BlockSpec((nlb, nrow, BLK), lambda i: (i, 0, 0))
    rows_f32 = jax.ShapeDtypeStruct((T // BLK, nrow, BLK), F32)
    tiles_bf16 = jax.ShapeDtypeStruct((T // BLK, nrow, BLK), BF16)
    return pl.pallas_call(
        _prep_kernel,
        grid=(T // PREP_TT,),
        in_specs=[pl.BlockSpec((PREP_TT, D), lambda i: (i, 0)),
                  _const_spec(wqt.shape), _const_spec(k1kh.shape), _const_spec(k2kh.shape),
                  _const_spec(k2hk.shape)],
        out_specs=[out_spec] * 4,
        out_shape=[rows_f32, rows_f32, tiles_bf16, tiles_bf16],
        scratch_shapes=[pltpu.VMEM((nlb, nrow, BLK), F32), pltpu.VMEM((nlb, nrow, BLK), F32),
                        pltpu.VMEM((nlb, nrow, BLK), F32), pltpu.VMEM((nlb, PEER_HEADS, BLK), F32)],
        compiler_params=pltpu.CompilerParams(
            dimension_semantics=("arbitrary",), vmem_limit_bytes=VMEM_LIMIT),
        name="peer_prep",
    )(xn, wqt, k1kh, k2kh, k2hk)


def _bf16_rows(row):
    return jnp.broadcast_to(row, (GATE_ROWS, BLK)).astype(BF16)


def _peer_stage(c_gate, xn_ref, thr_ref, c_ref, s2_ref, e2_ref, u_ref, vt_ref, yt_scr,
                ht_w, ht_r, at_w, at_r):
    n_i1 = PEER_EC // PEER_NKEYS
    nlb = PEER_TT // BLK
    yt_scr[...] += jnp.dot(vt_ref[0], at_r[...], preferred_element_type=F32)
    ht_w[...] = lax.dot_general(u_ref[...], xn_ref[...], (((1,), (1,)), ((), ())),
                                preferred_element_type=F32)
    row0 = c_gate * (n_i1 * PEER_HEADS)
    for lb in range(nlb):
        lanes = slice(lb * BLK, (lb + 1) * BLK)
        for rb in range(PEER_NKEYS // GATE_ROWS):
            gates = [jnp.zeros((GATE_ROWS, BLK), BF16) for _ in range(n_i1)]
            for hh in range(PEER_HEADS):
                rows = slice(hh * PEER_NKEYS + rb * GATE_ROWS, hh * PEER_NKEYS + (rb + 1) * GATE_ROWS)
                s2 = s2_ref[lb, rows, :]
                e2 = e2_ref[lb, rows, :]
                for ii in range(n_i1):
                    r = row0 + ii * PEER_HEADS + hh
                    thr = _bf16_rows(thr_ref[lb, pl.ds(r, 1), :])
                    cc = _bf16_rows(c_ref[lb, pl.ds(r, 1), :])
                    gates[ii] = gates[ii] + jnp.where(s2 >= thr, e2, jnp.zeros_like(e2)) * cc
            for ii in range(n_i1):
                rows = slice(ii * PEER_NKEYS + rb * GATE_ROWS, ii * PEER_NKEYS + (rb + 1) * GATE_ROWS)
                at_w[rows, lanes] = jax.nn.gelu(ht_r[rows, lanes]).astype(BF16) * gates[ii]


def _peer_kernel(xn_ref, thr_ref, c_ref, s2_in, e2_in, u_ref, vt_ref, x1_ref, fw_ref, out_ref,
                 yt_scr, ht_a, ht_b, at_a, at_b, s2_ref, e2_ref):
    s = pl.program_id(1)
    n_chunks = PEER_N_EXPERTS // PEER_EC

    @pl.when(s == 0)
    def _():
        yt_scr[...] = jnp.zeros_like(yt_scr)
        ht_a[...] = jnp.zeros_like(ht_a)
        ht_b[...] = jnp.zeros_like(ht_b)
        at_a[...] = jnp.zeros_like(at_a)
        at_b[...] = jnp.zeros_like(at_b)
        s2_ref[...] = s2_in[...]
        e2_ref[...] = e2_in[...]

    c_gate = jnp.clip(s - 1, 0, n_chunks - 1)
    stage = functools.partial(_peer_stage, c_gate, xn_ref, thr_ref, c_ref, s2_ref, e2_ref, u_ref,
                              vt_ref, yt_scr)
    even = lax.rem(s, 2) == 0

    @pl.when(even)
    def _():
        stage(ht_a, ht_b, at_b, at_a)

    @pl.when(jnp.logical_not(even))
    def _():
        stage(ht_b, ht_a, at_a, at_b)

    @pl.when(s == pl.num_programs(1) - 1)
    def _():
        xo = x1_ref[...] + yt_scr[...].T
        out_ref[...] = _rms(xo, fw_ref[...])


def _peer(xn, thr, cc, s2, e2, u, vt, x1, fw):
    T, D = xn.shape
    nrow = PEER_HEADS * PEER_NKEYS
    nlb = PEER_TT // BLK
    n_chunks = PEER_N_EXPERTS // PEER_EC
    prep_spec = pl.BlockSpec((nlb, nrow, BLK), lambda i, s: (i, 0, 0))
    tok_spec = pl.BlockSpec((PEER_TT, D), lambda i, s: (i, 0))
    return pl.pallas_call(
        _peer_kernel,
        grid=(T // PEER_TT, n_chunks + 2),
        in_specs=[tok_spec, prep_spec, prep_spec, prep_spec, prep_spec,
                  pl.BlockSpec((PEER_EC, D), lambda i, s: (jnp.minimum(s, n_chunks - 1), 0)),
                  pl.BlockSpec((1, D, PEER_EC), lambda i, s: (jnp.clip(s - 2, 0, n_chunks - 1), 0, 0)),
                  tok_spec, _const_spec(fw.shape)],
        out_specs=tok_spec,
        out_shape=jax.ShapeDtypeStruct((T, D), F32),
        scratch_shapes=[pltpu.VMEM((D, PEER_TT), F32),
                        pltpu.VMEM((PEER_EC, PEER_TT), F32), pltpu.VMEM((PEER_EC, PEER_TT), F32),
                        pltpu.VMEM((PEER_EC, PEER_TT), BF16), pltpu.VMEM((PEER_EC, PEER_TT), BF16),
                        pltpu.VMEM((nlb, nrow, BLK), BF16), pltpu.VMEM((nlb, nrow, BLK), BF16)],
        compiler_params=pltpu.CompilerParams(
            dimension_semantics=("arbitrary", "arbitrary"), vmem_limit_bytes=VMEM_LIMIT),
        name="peer_ffn",
    )(xn, thr, cc, s2, e2, u, vt, x1, fw)


def _t5_bucket(dist):
    n = np.maximum(dist, 0)
    max_exact = N_BUCKETS // 2
    large = max_exact + (np.log(np.maximum(n, 1) / max_exact) / np.log(MAX_DISTANCE / max_exact)
                         * (N_BUCKETS - max_exact)).astype(np.int32)
    large = np.minimum(large, N_BUCKETS - 1)
    return np.where(n < max_exact, n, large).astype(np.int32)


_PAIR_HEADS = [h for jj in range(Q_PER_KV) for h in (jj, Q_PER_KV + jj)]


def _band_bias(rel_bias):
    qi = np.arange(BLK)[:, None]
    kj = np.arange(2 * BLK)[None, :]
    dist = qi + BLK - kj
    in_window = (dist >= 0) & (dist < WINDOW)
    bias = rel_bias.astype(F32)[_t5_bucket(dist)]
    bias = jnp.where(in_window[:, :, None], bias, NEG_INF)
    bias = bias.transpose(2, 0, 1)[np.array(_PAIR_HEADS)]
    return bias.reshape(N_Q_HEADS * BLK, 2 * BLK)


def kernel(x, mix_norm_w, w_in, rel_bias, attn_sinks, w_branch_attn, gmlp_ln_w, gmlp_ln_b, gmlp_w_s,
           gmlp_b_s, w_branch_gmlp, w_out, ffn_norm_w, peer_w_q, peer_keys_1, peer_keys_2, peer_u,
           peer_v, final_norm_w):
    B, S, D = x.shape
    assert w_in.shape[0] == 1, "single-layer block"
    l = 0
    bias = _band_bias(rel_bias)
    causal = np.tril(np.ones((BLK, BLK), dtype=bool))
    eye = jnp.eye(PEER_HEADS, dtype=F32)
    nrow = PEER_HEADS * PEER_NKEYS

    w_q_pairs = (w_in[l][:, :ATTN_WIDTH].reshape(D, N_KV_HEADS, Q_PER_KV, HEAD_DIM)
                 .transpose(0, 2, 1, 3).reshape(D, ATTN_WIDTH))
    win = jnp.concatenate([w_q_pairs, w_in[l][:, ATTN_WIDTH:]], axis=1).astype(BF16)
    wba = (w_branch_attn[l].reshape(N_KV_HEADS, Q_PER_KV, HEAD_DIM, D)
           .transpose(1, 0, 2, 3).reshape(ATTN_WIDTH, D).astype(BF16))
    sinks = attn_sinks[l].astype(F32).reshape(N_KV_HEADS, Q_PER_KV).T.reshape(N_Q_HEADS)
    ws = jnp.where(causal[None], gmlp_w_s[l], 0).astype(BF16)
    btab = jnp.repeat(gmlp_b_s[l].astype(F32).T, GMLP_WIDTH // GMLP_GROUPS, axis=1)
    x1, xn = _mixer(x, sinks, mix_norm_w[l].reshape(1, D), win, bias, wba,
                    gmlp_ln_w[l].reshape(1, -1), gmlp_ln_b[l].reshape(1, -1), ws, btab,
                    w_branch_gmlp[l].astype(BF16), w_out[l].astype(BF16), ffn_norm_w[l].reshape(1, D))

    wqt = peer_w_q[l].reshape(D, PEER_HEADS, 2, PEER_HALF).transpose(2, 1, 3, 0).reshape(2 * nrow, D)
    k1kh = jnp.einsum('kd,hg->khgd', peer_keys_1[l], eye).reshape(nrow, nrow).astype(BF16)
    k2kh = jnp.einsum('kd,hg->khgd', peer_keys_2[l], eye).reshape(nrow, nrow).astype(BF16)
    k2hk = jnp.einsum('kd,hg->hkgd', peer_keys_2[l], eye).reshape(nrow, nrow).astype(BF16)
    xn2 = xn.reshape(B * S, D)
    thr, cc, s2, e2 = _peer_prep(xn2, wqt.astype(BF16), k1kh, k2kh, k2hk)
    vt = peer_v[l].reshape(PEER_N_EXPERTS // PEER_EC, PEER_EC, D).transpose(0, 2, 1).astype(BF16)
    y = _peer(xn2, thr, cc, s2, e2, peer_u[l].astype(BF16), vt,
              x1.reshape(B * S, D), final_norm_w.reshape(1, D))
    return y.reshape(B, S, D)
```

```python
import functools

import numpy as np
import jax
import jax.numpy as jnp
from jax import lax
from jax.experimental import pallas as pl
from jax.experimental.pallas import tpu as pltpu

F32 = jnp.float32
BF16 = jnp.bfloat16

D_MODEL = 1024
HEAD_DIM = 64
N_Q_HEADS = 8
N_KV_HEADS = 2
Q_PER_KV = N_Q_HEADS // N_KV_HEADS
ATTN_WIDTH = N_Q_HEADS * HEAD_DIM
KV_WIDTH = N_KV_HEADS * HEAD_DIM
WINDOW = 128
BLK = 128
N_BUCKETS = 32
MAX_DISTANCE = 128
GMLP_GROUPS = 8
GMLP_WIDTH = 512
PEER_HEADS = 8
PEER_NKEYS = 128
PEER_N_EXPERTS = PEER_NKEYS * PEER_NKEYS
PEER_HALF = 128
PEER_TOPK = 16
EPS = 1e-6
NEG_INF = -1e30

_C_Q, _C_K, _C_V, _C_GU, _C_GV, _C_GA, _C_GB, _C_END = 0, 512, 640, 768, 1280, 1792, 2816, 3840

MIX_TM = 512
PREP_TT = 512
PEER_TT = 512
PEER_EC = 512
GATE_ROWS = 32
VMEM_LIMIT = 56 * 1024 * 1024


def _rms(x, w):
    return x * lax.rsqrt(jnp.mean(x * x, axis=-1, keepdims=True) + EPS) * w


def _mixer_kernel(sink_ref, x_ref, nw_ref, win_ref, bias_ref, wba_ref, lnw_ref, lnb_ref, ws_ref,
                  btab_ref, wbg_ref, wout_ref, fnw_ref, x1_ref, xn_ref,
                  q_scr, kext, vext, gu_scr, gv_scr, ya_scr, yb_scr):
    j = pl.program_id(1)
    nb = MIX_TM // BLK
    x = x_ref[0]
    h = _rms(x, nw_ref[...]).astype(BF16)

    def proj(lo, hi):
        return jnp.dot(h, win_ref[:, lo:hi], preferred_element_type=F32)

    q_scr[...] = proj(_C_Q, _C_K) * (HEAD_DIM ** -0.5)

    @pl.when(j == 0)
    def _():
        kext[0:BLK, :] = jnp.zeros((BLK, KV_WIDTH), BF16)
        vext[0:BLK, :] = jnp.zeros((BLK, KV_WIDTH), BF16)

    kext[BLK:, :] = proj(_C_K, _C_V).astype(BF16)
    vext[BLK:, :] = proj(_C_V, _C_GU).astype(BF16)
    gu_scr[...] = jax.nn.gelu(proj(_C_GU, _C_GV))
    gv = jax.nn.gelu(proj(_C_GV, _C_GA))
    mu = jnp.mean(gv, axis=-1, keepdims=True)
    gc = gv - mu
    gvn = gc * lax.rsqrt(jnp.mean(gc * gc, axis=-1, keepdims=True) + EPS)
    gv_scr[...] = (gvn * lnw_ref[...] + lnb_ref[...]).astype(BF16)

    lane = lax.broadcasted_iota(jnp.int32, (BLK, BLK), 1)
    low = lane < HEAD_DIM
    col = lax.broadcasted_iota(jnp.int32, (BLK, 2 * BLK), 1)

    def block(b, carry):
        r0 = pl.multiple_of(b * BLK, BLK)
        qb = q_scr[pl.ds(r0, BLK), :]
        kb = kext[pl.ds(r0, 2 * BLK), :]
        vb = vext[pl.ds(r0, 2 * BLK), :]
        parts = []
        for jj in range(4):
            p_j = qb[:, jj * BLK:(jj + 1) * BLK]
            parts.append(jnp.where(low, p_j, 0.0).astype(BF16))
            parts.append(jnp.where(low, 0.0, p_j).astype(BF16))
        qs = jnp.concatenate(parts, axis=0)
        logits = lax.dot_general(qs, kb, (((1,), (1,)), ((), ())), preferred_element_type=F32)
        left_cut = jnp.where(j * nb + b == 0, BLK, 0)
        outs = []
        for r in range(8):
            lg = logits[r * BLK:(r + 1) * BLK, :] + bias_ref[r * BLK:(r + 1) * BLK, :]
            lg = jnp.where(col < left_cut, NEG_INF, lg)
            s = sink_ref[r]
            m = jnp.maximum(jnp.max(lg, axis=-1, keepdims=True), s)
            p = jnp.exp(lg - m)
            den = jnp.sum(p, axis=-1, keepdims=True) + jnp.exp(s - m)
            pv = jnp.dot(p.astype(BF16), vb, preferred_element_type=F32)
            outs.append(pv / den)
        pairs = [jnp.where(low, outs[2 * jj], outs[2 * jj + 1]) for jj in range(4)]
        ya_scr[pl.ds(r0, BLK), :] = jnp.concatenate(pairs, axis=1).astype(BF16)

        gub = gu_scr[pl.ds(r0, BLK), :]
        vnb = gv_scr[pl.ds(r0, BLK), :]
        ys = []
        for mm in range(4):
            vblk = vnb[:, mm * BLK:(mm + 1) * BLK]
            s0 = jnp.dot(ws_ref[2 * mm], vblk, preferred_element_type=F32)
            s1 = jnp.dot(ws_ref[2 * mm + 1], vblk, preferred_element_type=F32)
            sg = jnp.where(low, s0, s1) + btab_ref[:, mm * BLK:(mm + 1) * BLK]
            ys.append(gub[:, mm * BLK:(mm + 1) * BLK] * sg)
        yb_scr[pl.ds(r0, BLK), :] = jnp.concatenate(ys, axis=1).astype(BF16)
        return carry

    lax.fori_loop(0, nb, block, 0)

    kext[0:BLK, :] = kext[MIX_TM:MIX_TM + BLK, :]
    vext[0:BLK, :] = vext[MIX_TM:MIX_TM + BLK, :]

    ya_p = jnp.dot(ya_scr[...], wba_ref[...], preferred_element_type=F32)
    yb_p = jnp.dot(yb_scr[...], wbg_ref[...], preferred_element_type=F32)
    merged = jax.nn.sigmoid(proj(_C_GA, _C_GB)) * ya_p + jax.nn.sigmoid(proj(_C_GB, _C_END)) * yb_p
    x1 = x + jnp.dot(merged.astype(BF16), wout_ref[...], preferred_element_type=F32)
    x1_ref[0] = x1
    xn_ref[0] = _rms(x1, fnw_ref[...]).astype(BF16)


def _const_spec(shape):
    nd = len(shape)
    return pl.BlockSpec(shape, lambda *_: (0,) * nd)


def _mixer(x, sinks, nw, win, bias, wba, lnw, lnb, ws, btab, wbg, wout, fnw):
    B, S, D = x.shape
    grid = (B, S // MIX_TM)
    tile = pl.BlockSpec((1, MIX_TM, D), lambda b, j: (b, j, 0))
    in_specs = [
        pl.BlockSpec(memory_space=pltpu.SMEM),
        tile,
        _const_spec(nw.shape), _const_spec(win.shape), _const_spec(bias.shape), _const_spec(wba.shape),
        _const_spec(lnw.shape), _const_spec(lnb.shape), _const_spec(ws.shape), _const_spec(btab.shape),
        _const_spec(wbg.shape), _const_spec(wout.shape), _const_spec(fnw.shape),
    ]
    return pl.pallas_call(
        _mixer_kernel,
        grid=grid,
        in_specs=in_specs,
        out_specs=[tile, tile],
        out_shape=[jax.ShapeDtypeStruct((B, S, D), F32), jax.ShapeDtypeStruct((B, S, D), BF16)],
        scratch_shapes=[
            pltpu.VMEM((MIX_TM, ATTN_WIDTH), F32),
            pltpu.VMEM((MIX_TM + BLK, KV_WIDTH), BF16),
            pltpu.VMEM((MIX_TM + BLK, KV_WIDTH), BF16),
            pltpu.VMEM((MIX_TM, GMLP_WIDTH), F32),
            pltpu.VMEM((MIX_TM, GMLP_WIDTH), BF16),
            pltpu.VMEM((MIX_TM, ATTN_WIDTH), BF16),
            pltpu.VMEM((MIX_TM, GMLP_WIDTH), BF16),
        ],
        compiler_params=pltpu.CompilerParams(
            dimension_semantics=("arbitrary", "arbitrary"), vmem_limit_bytes=VMEM_LIMIT),
        name="mixer",
    )(sinks, x, nw, win, bias, wba, lnw, lnb, ws, btab, wbg, wout, fnw)


def _cmpx(a, b):
    if a is None:
        return b, None
    if b is None:
        return a, None
    return jnp.maximum(a, b), jnp.minimum(a, b)


def _bitonic_sort_desc(v):
    v = list(v)
    n = len(v)
    k = 2
    while k <= n:
        jj = k // 2
        while jj >= 1:
            for i in range(n):
                l = i ^ jj
                if l > i:
                    hi, lo = _cmpx(v[i], v[l])
                    if (i & k) == 0:
                        v[i], v[l] = hi, lo
                    else:
                        v[i], v[l] = lo, hi
            jj //= 2
        k *= 2
    return v


def _merge_top(a, b):
    n = len(a)
    c = [_cmpx(a[i], b[n - 1 - i])[0] for i in range(n)]
    jj = n // 2
    while jj >= 1:
        for i in range(n):
            if (i & jj) == 0:
                c[i], c[i + jj] = _cmpx(c[i], c[i + jj])
        jj //= 2
    return c


def _top_sorted(vals, k):
    vals = list(vals)
    while len(vals) % k:
        vals.append(None)
    groups = [_bitonic_sort_desc(vals[g:g + k]) for g in range(0, len(vals), k)]
    while len(groups) > 1:
        nxt = [_merge_top(groups[g], groups[g + 1]) for g in range(0, len(groups) - 1, 2)]
        if len(groups) % 2:
            nxt.append(groups[-1])
        groups = nxt
    return groups[0]


def _max_below(vals, bound):
    m = None
    for v in vals:
        w = jnp.where(v < bound, v, -3.0e38)
        m = w if m is None else jnp.maximum(m, w)
    return m


def _prep_kernel(xn_ref, wqt_ref, k1_ref, k2kh_ref, k2hk_ref, thr_ref, c_ref, s2_ref, e2_ref,
                 s1_scr, s2kh_scr, s2hk_scr, b0_scr):
    nlb = PREP_TT // BLK
    xn = xn_ref[...]
    qt = lax.dot_general(wqt_ref[...], xn, (((1,), (1,)), ((), ())), preferred_element_type=F32)
    nrow = PEER_HEADS * PEER_HALF
    q1 = qt[:nrow].astype(BF16)
    q2 = qt[nrow:].astype(BF16)
    s1 = jnp.dot(k1_ref[...], q1, preferred_element_type=F32)
    s2kh = jnp.dot(k2kh_ref[...], q2, preferred_element_type=F32)
    s2hk = jnp.dot(k2hk_ref[...], q2, preferred_element_type=F32)
    for lb in range(nlb):
        s1_scr[lb] = s1[:, lb * BLK:(lb + 1) * BLK]
        s2kh_scr[lb] = s2kh[:, lb * BLK:(lb + 1) * BLK]
        s2hk_scr[lb] = s2hk[:, lb * BLK:(lb + 1) * BLK]

    def lane_block(lb, carry):
        s1_rows = [s1_scr[lb, k * 8:(k + 1) * 8, :] for k in range(PEER_NKEYS)]
        s2_rows = [s2kh_scr[lb, k * 8:(k + 1) * 8, :] for k in range(PEER_NKEYS)]
        a = _top_sorted(s1_rows, PEER_TOPK)
        b = _top_sorted(s2_rows, PEER_TOPK)
        a_next = _max_below(s1_rows, a[PEER_TOPK - 1])
        b_next = _max_below(s2_rows, b[PEER_TOPK - 1])
        cands = [a[i] + b[jx] for i in range(PEER_TOPK) for jx in range(PEER_TOPK)
                 if (i + 1) * (jx + 1) <= PEER_TOPK]
        top = _top_sorted(cands, PEER_TOPK)
        runner_up = _max_below(cands + [a_next + b[0], a[0] + b_next], top[PEER_TOPK - 1])
        tau = 0.5 * (top[PEER_TOPK - 1] + runner_up)
        z = jnp.ones_like(tau)
        for kk in range(1, PEER_TOPK):
            z = z + jnp.exp(top[kk] - top[0])
        inv_z = 1.0 / z
        tau_rel = tau - b[0]
        for k in range(PEER_NKEYS):
            s1k = s1_scr[lb, k * 8:(k + 1) * 8, :]
            thr_ref[lb, k * 8:(k + 1) * 8, :] = tau_rel - s1k
            c_ref[lb, k * 8:(k + 1) * 8, :] = jnp.exp(s1k - a[0]) * inv_z
        b0_scr[lb] = b[0]
        return carry

    lax.fori_loop(0, nlb, lane_block, 0)

    for lb in range(nlb):
        for hh in range(PEER_HEADS):
            rows = slice(hh * PEER_NKEYS, (hh + 1) * PEER_NKEYS)
            s2_rel = s2hk_scr[lb, rows, :] - b0_scr[lb, hh:hh + 1, :]
            s2_ref[lb, rows, :] = s2_rel
            e2_ref[lb, rows, :] = jnp.exp(s2_rel)


def _peer_prep(xn, wqt, k1kh, k2kh, k2hk):
    T, D = xn.shape
    nrow = PEER_HEADS * PEER_NKEYS
    nlb = PREP_TT // BLK
    out_spec = pl.BlockSpec((nlb, nrow, BLK), lambda i: (i, 0, 0))
    rows_f32 = jax.ShapeDtypeStruct((T // BLK, nrow, BLK), F32)
    return pl.pallas_call(
        _prep_kernel,
        grid=(T // PREP_TT,),
        in_specs=[pl.BlockSpec((PREP_TT, D), lambda i: (i, 0)),
                  _const_spec(wqt.shape), _const_spec(k1kh.shape), _const_spec(k2kh.shape),
                  _const_spec(k2hk.shape)],
        out_specs=[out_spec] * 4,
        out_shape=[rows_f32] * 4,
        scratch_shapes=[pltpu.VMEM((nlb, nrow, BLK), F32), pltpu.VMEM((nlb, nrow, BLK), F32),
                        pltpu.VMEM((nlb, nrow, BLK), F32), pltpu.VMEM((nlb, PEER_HEADS, BLK), F32)],
        compiler_params=pltpu.CompilerParams(
            dimension_semantics=("arbitrary",), vmem_limit_bytes=VMEM_LIMIT),
        name="peer_prep",
    )(xn, wqt, k1kh, k2kh, k2hk)


def _peer_stage(c_gate, xn_ref, thr_ref, c_ref, s2_ref, e2_ref, u_ref, vt_ref, yt_scr,
                ht_w, ht_r, at_w, at_r, *, first_matmul=True, gate=True, second_matmul=True):
    n_i1 = PEER_EC // PEER_NKEYS
    nlb = PEER_TT // BLK
    if second_matmul:
        yt_scr[...] += jnp.dot(vt_ref[0], at_r[...], preferred_element_type=F32)
    if first_matmul:
        ht_w[...] = lax.dot_general(u_ref[...], xn_ref[...], (((1,), (1,)), ((), ())),
                                    preferred_element_type=F32)
    if not gate:
        return
    row0 = c_gate * (n_i1 * PEER_HEADS)
    for lb in range(nlb):
        lanes = slice(lb * BLK, (lb + 1) * BLK)
        for rb in range(PEER_NKEYS // GATE_ROWS):
            gates = [jnp.zeros((GATE_ROWS, BLK), F32) for _ in range(n_i1)]
            for hh in range(PEER_HEADS):
                rows = slice(hh * PEER_NKEYS + rb * GATE_ROWS, hh * PEER_NKEYS + (rb + 1) * GATE_ROWS)
                s2 = s2_ref[lb, rows, :]
                e2 = e2_ref[lb, rows, :]
                for ii in range(n_i1):
                    r = row0 + ii * PEER_HEADS + hh
                    thr = thr_ref[lb, pl.ds(r, 1), :]
                    cc = c_ref[lb, pl.ds(r, 1), :]
                    gates[ii] = gates[ii] + jnp.where(s2 >= thr, e2, 0.0) * cc
            for ii in range(n_i1):
                rows = slice(ii * PEER_NKEYS + rb * GATE_ROWS, ii * PEER_NKEYS + (rb + 1) * GATE_ROWS)
                at_w[rows, lanes] = (jax.nn.gelu(ht_r[rows, lanes]) * gates[ii]).astype(BF16)


def _peer_kernel(xn_ref, thr_ref, c_ref, s2_ref, e2_ref, u_ref, vt_ref, x1_ref, fw_ref, out_ref,
                 yt_scr, ht_a, ht_b, at_a, at_b):
    s = pl.program_id(1)
    n_chunks = PEER_N_EXPERTS // PEER_EC

    @pl.when(s == 0)
    def _():
        yt_scr[...] = jnp.zeros_like(yt_scr)

    c_gate = jnp.clip(s - 1, 0, n_chunks - 1)
    stage = functools.partial(_peer_stage, c_gate, xn_ref, thr_ref, c_ref, s2_ref, e2_ref, u_ref,
                              vt_ref, yt_scr)
    even = lax.rem(s, 2) == 0
    inner = jnp.logical_and(s >= 2, s < n_chunks)

    @pl.when(jnp.logical_and(even, inner))
    def _():
        stage(ht_a, ht_b, at_b, at_a)

    @pl.when(jnp.logical_and(jnp.logical_not(even), inner))
    def _():
        stage(ht_b, ht_a, at_a, at_b)

    @pl.when(s == 0)
    def _():
        stage(ht_a, ht_b, at_b, at_a, gate=False, second_matmul=False)

    @pl.when(s == 1)
    def _():
        stage(ht_b, ht_a, at_a, at_b, second_matmul=False)

    @pl.when(s == n_chunks)
    def _():
        stage(ht_a, ht_b, at_b, at_a, first_matmul=False)

    @pl.when(s == n_chunks + 1)
    def _():
        stage(ht_b, ht_a, at_a, at_b, first_matmul=False, gate=False)

    @pl.when(s == pl.num_programs(1) - 1)
    def _():
        xo = x1_ref[...] + yt_scr[...].T
        out_ref[...] = _rms(xo, fw_ref[...])


def _peer(xn, thr, cc, s2, e2, u, vt, x1, fw):
    T, D = xn.shape
    nrow = PEER_HEADS * PEER_NKEYS
    nlb = PEER_TT // BLK
    n_chunks = PEER_N_EXPERTS // PEER_EC
    assert n_chunks % 2 == 0 and n_chunks >= 4
    prep_spec = pl.BlockSpec((nlb, nrow, BLK), lambda i, s: (i, 0, 0))
    tok_spec = pl.BlockSpec((PEER_TT, D), lambda i, s: (i, 0))
    return pl.pallas_call(
        _peer_kernel,
        grid=(T // PEER_TT, n_chunks + 2),
        in_specs=[tok_spec, prep_spec, prep_spec, prep_spec, prep_spec,
                  pl.BlockSpec((PEER_EC, D), lambda i, s: (jnp.minimum(s, n_chunks - 1), 0)),
                  pl.BlockSpec((1, D, PEER_EC), lambda i, s: (jnp.clip(s - 2, 0, n_chunks - 1), 0, 0)),
                  tok_spec, _const_spec(fw.shape)],
        out_specs=tok_spec,
        out_shape=jax.ShapeDtypeStruct((T, D), F32),
        scratch_shapes=[pltpu.VMEM((D, PEER_TT), F32),
                        pltpu.VMEM((PEER_EC, PEER_TT), F32), pltpu.VMEM((PEER_EC, PEER_TT), F32),
                        pltpu.VMEM((PEER_EC, PEER_TT), BF16), pltpu.VMEM((PEER_EC, PEER_TT), BF16)],
        compiler_params=pltpu.CompilerParams(
            dimension_semantics=("arbitrary", "arbitrary"), vmem_limit_bytes=VMEM_LIMIT),
        name="peer_ffn",
    )(xn, thr, cc, s2, e2, u, vt, x1, fw)


def _t5_bucket(dist):
    n = np.maximum(dist, 0)
    max_exact = N_BUCKETS // 2
    large = max_exact + (np.log(np.maximum(n, 1) / max_exact) / np.log(MAX_DISTANCE / max_exact)
                         * (N_BUCKETS - max_exact)).astype(np.int32)
    large = np.minimum(large, N_BUCKETS - 1)
    return np.where(n < max_exact, n, large).astype(np.int32)


_PAIR_HEADS = [h for jj in range(Q_PER_KV) for h in (jj, Q_PER_KV + jj)]


def _band_bias(rel_bias):
    qi = np.arange(BLK)[:, None]
    kj = np.arange(2 * BLK)[None, :]
    dist = qi + BLK - kj
    in_window = (dist >= 0) & (dist < WINDOW)
    onehot = (_t5_bucket(dist)[:, :, None] == np.arange(N_BUCKETS)).astype(np.float32)
    per_head = rel_bias.astype(F32)[:, np.array(_PAIR_HEADS)]
    bias = jnp.einsum('qkb,bh->hqk', onehot, per_head, precision=lax.Precision.HIGHEST)
    bias = jnp.where(in_window[None], bias, NEG_INF)
    return bias.reshape(N_Q_HEADS * BLK, 2 * BLK)


def kernel(x, mix_norm_w, w_in, rel_bias, attn_sinks, w_branch_attn, gmlp_ln_w, gmlp_ln_b, gmlp_w_s,
           gmlp_b_s, w_branch_gmlp, w_out, ffn_norm_w, peer_w_q, peer_keys_1, peer_keys_2, peer_u,
           peer_v, final_norm_w):
    B, S, D = x.shape
    assert w_in.shape[0] == 1, "single-layer block"
    l = 0
    bias = _band_bias(rel_bias)
    causal = np.tril(np.ones((BLK, BLK), dtype=bool))
    eye = jnp.eye(PEER_HEADS, dtype=F32)
    nrow = PEER_HEADS * PEER_NKEYS

    w_q_pairs = (w_in[l][:, :ATTN_WIDTH].reshape(D, N_KV_HEADS, Q_PER_KV, HEAD_DIM)
                 .transpose(0, 2, 1, 3).reshape(D, ATTN_WIDTH))
    win = jnp.concatenate([w_q_pairs, w_in[l][:, ATTN_WIDTH:]], axis=1).astype(BF16)
    wba = (w_branch_attn[l].reshape(N_KV_HEADS, Q_PER_KV, HEAD_DIM, D)
           .transpose(1, 0, 2, 3).reshape(ATTN_WIDTH, D).astype(BF16))
    sinks = attn_sinks[l].astype(F32).reshape(N_KV_HEADS, Q_PER_KV).T.reshape(N_Q_HEADS)
    ws = jnp.where(causal[None], gmlp_w_s[l], 0).astype(BF16)
    btab = jnp.repeat(gmlp_b_s[l].astype(F32).T, GMLP_WIDTH // GMLP_GROUPS, axis=1)
    x1, xn = _mixer(x, sinks, mix_norm_w[l].reshape(1, D), win, bias, wba,
                    gmlp_ln_w[l].reshape(1, -1), gmlp_ln_b[l].reshape(1, -1), ws, btab,
                    w_branch_gmlp[l].astype(BF16), w_out[l].astype(BF16), ffn_norm_w[l].reshape(1, D))

    wqt = peer_w_q[l].reshape(D, PEER_HEADS, 2, PEER_HALF).transpose(2, 1, 3, 0).reshape(2 * nrow, D)
    k1kh = jnp.einsum('kd,hg->khgd', peer_keys_1[l], eye).reshape(nrow, nrow).astype(BF16)
    k2kh = jnp.einsum('kd,hg->khgd', peer_keys_2[l], eye).reshape(nrow, nrow).astype(BF16)
    k2hk = jnp.einsum('kd,hg->hkgd', peer_keys_2[l], eye).reshape(nrow, nrow).astype(BF16)
    xn2 = xn.reshape(B * S, D)
    thr, cc, s2, e2 = _peer_prep(xn2, wqt.astype(BF16), k1kh, k2kh, k2hk)
    vt = peer_v[l].reshape(PEER_N_EXPERTS // PEER_EC, PEER_EC, D).transpose(0, 2, 1).astype(BF16)
    y = _peer(xn2, thr, cc, s2, e2, peer_u[l].astype(BF16), vt,
              x1.reshape(B * S, D), final_norm_w.reshape(1, D))
    return y.reshape(B, S, D)
```

```python
import functools

import numpy as np
import jax
import jax.numpy as jnp
from jax import lax
from jax.experimental import pallas as pl
from jax.experimental.pallas import tpu as pltpu

F32 = jnp.float32
BF16 = jnp.bfloat16

D_MODEL = 1024
HEAD_DIM = 64
N_Q_HEADS = 8
N_KV_HEADS = 2
Q_PER_KV = N_Q_HEADS // N_KV_HEADS
ATTN_WIDTH = N_Q_HEADS * HEAD_DIM
KV_WIDTH = N_KV_HEADS * HEAD_DIM
WINDOW = 128
BLK = 128
N_BUCKETS = 32
MAX_DISTANCE = 128
GMLP_GROUPS = 8
GMLP_WIDTH = 512
PEER_HEADS = 8
PEER_NKEYS = 128
PEER_N_EXPERTS = PEER_NKEYS * PEER_NKEYS
PEER_HALF = 128
PEER_TOPK = 16
EPS = 1e-6
NEG_INF = -1e30

_C_Q, _C_K, _C_V, _C_GU, _C_GV, _C_GA, _C_GB, _C_END = 0, 512, 640, 768, 1280, 1792, 2816, 3840

MIX_TM = 512
PREP_TT = 512
PEER_TT = 512
PEER_EC = 1024
GATE_ROWS = 32
VMEM_LIMIT = 56 * 1024 * 1024


def _rms(x, w):
    return x * lax.rsqrt(jnp.mean(x * x, axis=-1, keepdims=True) + EPS) * w


def _mixer_kernel(sink_ref, x_ref, nw_ref, win_ref, bias_ref, wba_ref, lnw_ref, lnb_ref, ws_ref,
                  btab_ref, wbg_ref, wout_ref, fnw_ref, x1_ref, xn_ref,
                  q_scr, kext, vext, gu_scr, gv_scr, ya_scr, yb_scr):
    j = pl.program_id(1)
    nb = MIX_TM // BLK
    x = x_ref[0]
    h = _rms(x, nw_ref[...]).astype(BF16)

    def proj(lo, hi):
        return jnp.dot(h, win_ref[:, lo:hi], preferred_element_type=F32)

    q_scr[...] = proj(_C_Q, _C_K) * (HEAD_DIM ** -0.5)

    @pl.when(j == 0)
    def _():
        kext[0:BLK, :] = jnp.zeros((BLK, KV_WIDTH), BF16)
        vext[0:BLK, :] = jnp.zeros((BLK, KV_WIDTH), BF16)

    kext[BLK:, :] = proj(_C_K, _C_V).astype(BF16)
    vext[BLK:, :] = proj(_C_V, _C_GU).astype(BF16)
    gu_scr[...] = jax.nn.gelu(proj(_C_GU, _C_GV))
    gv = jax.nn.gelu(proj(_C_GV, _C_GA))
    mu = jnp.mean(gv, axis=-1, keepdims=True)
    gc = gv - mu
    gvn = gc * lax.rsqrt(jnp.mean(gc * gc, axis=-1, keepdims=True) + EPS)
    gv_scr[...] = (gvn * lnw_ref[...] + lnb_ref[...]).astype(BF16)

    lane = lax.broadcasted_iota(jnp.int32, (BLK, BLK), 1)
    low = lane < HEAD_DIM
    col = lax.broadcasted_iota(jnp.int32, (BLK, 2 * BLK), 1)

    def block(b, carry):
        r0 = pl.multiple_of(b * BLK, BLK)
        qb = q_scr[pl.ds(r0, BLK), :]
        kb = kext[pl.ds(r0, 2 * BLK), :]
        vb = vext[pl.ds(r0, 2 * BLK), :]
        parts = []
        for jj in range(4):
            p_j = qb[:, jj * BLK:(jj + 1) * BLK]
            parts.append(jnp.where(low, p_j, 0.0).astype(BF16))
            parts.append(jnp.where(low, 0.0, p_j).astype(BF16))
        qs = jnp.concatenate(parts, axis=0)
        logits = lax.dot_general(qs, kb, (((1,), (1,)), ((), ())), preferred_element_type=F32)
        left_cut = jnp.where(j * nb + b == 0, BLK, 0)
        outs = []
        for r in range(8):
            lg = logits[r * BLK:(r + 1) * BLK, :] + bias_ref[r * BLK:(r + 1) * BLK, :]
            lg = jnp.where(col < left_cut, NEG_INF, lg)
            s = sink_ref[r]
            m = jnp.maximum(jnp.max(lg, axis=-1, keepdims=True), s)
            p = jnp.exp(lg - m)
            den = jnp.sum(p, axis=-1, keepdims=True) + jnp.exp(s - m)
            pv = jnp.dot(p.astype(BF16), vb, preferred_element_type=F32)
            outs.append(pv / den)
        pairs = [jnp.where(low, outs[2 * jj], outs[2 * jj + 1]) for jj in range(4)]
        ya_scr[pl.ds(r0, BLK), :] = jnp.concatenate(pairs, axis=1).astype(BF16)

        gub = gu_scr[pl.ds(r0, BLK), :]
        vnb = gv_scr[pl.ds(r0, BLK), :]
        ys = []
        for mm in range(4):
            vblk = vnb[:, mm * BLK:(mm + 1) * BLK]
            s0 = jnp.dot(ws_ref[2 * mm], vblk, preferred_element_type=F32)
            s1 = jnp.dot(ws_ref[2 * mm + 1], vblk, preferred_element_type=F32)
            sg = jnp.where(low, s0, s1) + btab_ref[:, mm * BLK:(mm + 1) * BLK]
            ys.append(gub[:, mm * BLK:(mm + 1) * BLK] * sg)
        yb_scr[pl.ds(r0, BLK), :] = jnp.concatenate(ys, axis=1).astype(BF16)
        return carry

    lax.fori_loop(0, nb, block, 0)

    kext[0:BLK, :] = kext[MIX_TM:MIX_TM + BLK, :]
    vext[0:BLK, :] = vext[MIX_TM:MIX_TM + BLK, :]

    ya_p = jnp.dot(ya_scr[...], wba_ref[...], preferred_element_type=F32)
    yb_p = jnp.dot(yb_scr[...], wbg_ref[...], preferred_element_type=F32)
    merged = jax.nn.sigmoid(proj(_C_GA, _C_GB)) * ya_p + jax.nn.sigmoid(proj(_C_GB, _C_END)) * yb_p
    x1 = x + jnp.dot(merged.astype(BF16), wout_ref[...], preferred_element_type=F32)
    x1_ref[0] = x1
    xn_ref[0] = _rms(x1, fnw_ref[...]).astype(BF16)


def _const_spec(shape):
    nd = len(shape)
    return pl.BlockSpec(shape, lambda *_: (0,) * nd)


def _mixer(x, sinks, nw, win, bias, wba, lnw, lnb, ws, btab, wbg, wout, fnw):
    B, S, D = x.shape
    grid = (B, S // MIX_TM)
    tile = pl.BlockSpec((1, MIX_TM, D), lambda b, j: (b, j, 0))
    in_specs = [
        pl.BlockSpec(memory_space=pltpu.SMEM),
        tile,
        _const_spec(nw.shape), _const_spec(win.shape), _const_spec(bias.shape), _const_spec(wba.shape),
        _const_spec(lnw.shape), _const_spec(lnb.shape), _const_spec(ws.shape), _const_spec(btab.shape),
        _const_spec(wbg.shape), _const_spec(wout.shape), _const_spec(fnw.shape),
    ]
    return pl.pallas_call(
        _mixer_kernel,
        grid=grid,
        in_specs=in_specs,
        out_specs=[tile, tile],
        out_shape=[jax.ShapeDtypeStruct((B, S, D), F32), jax.ShapeDtypeStruct((B, S, D), BF16)],
        scratch_shapes=[
            pltpu.VMEM((MIX_TM, ATTN_WIDTH), F32),
            pltpu.VMEM((MIX_TM + BLK, KV_WIDTH), BF16),
            pltpu.VMEM((MIX_TM + BLK, KV_WIDTH), BF16),
            pltpu.VMEM((MIX_TM, GMLP_WIDTH), F32),
            pltpu.VMEM((MIX_TM, GMLP_WIDTH), BF16),
            pltpu.VMEM((MIX_TM, ATTN_WIDTH), BF16),
            pltpu.VMEM((MIX_TM, GMLP_WIDTH), BF16),
        ],
        compiler_params=pltpu.CompilerParams(
            dimension_semantics=("arbitrary", "arbitrary"), vmem_limit_bytes=VMEM_LIMIT),
        name="mixer",
    )(sinks, x, nw, win, bias, wba, lnw, lnb, ws, btab, wbg, wout, fnw)


def _cmpx(a, b):
    if a is None:
        return b, None
    if b is None:
        return a, None
    return jnp.maximum(a, b), jnp.minimum(a, b)


def _bitonic_sort_desc(v):
    v = list(v)
    n = len(v)
    k = 2
    while k <= n:
        jj = k // 2
        while jj >= 1:
            for i in range(n):
                l = i ^ jj
                if l > i:
                    hi, lo = _cmpx(v[i], v[l])
                    if (i & k) == 0:
                        v[i], v[l] = hi, lo
                    else:
                        v[i], v[l] = lo, hi
            jj //= 2
        k *= 2
    return v


def _merge_top(a, b):
    n = len(a)
    c = [_cmpx(a[i], b[n - 1 - i])[0] for i in range(n)]
    jj = n // 2
    while jj >= 1:
        for i in range(n):
            if (i & jj) == 0:
                c[i], c[i + jj] = _cmpx(c[i], c[i + jj])
        jj //= 2
    return c


def _top_sorted(vals, k):
    vals = list(vals)
    while len(vals) % k:
        vals.append(None)
    groups = [_bitonic_sort_desc(vals[g:g + k]) for g in range(0, len(vals), k)]
    while len(groups) > 1:
        nxt = [_merge_top(groups[g], groups[g + 1]) for g in range(0, len(groups) - 1, 2)]
        if len(groups) % 2:
            nxt.append(groups[-1])
        groups = nxt
    return groups[0]


def _max_below(vals, bound):
    m = None
    for v in vals:
        w = jnp.where(v < bound, v, -3.0e38)
        m = w if m is None else jnp.maximum(m, w)
    return m


def _prep_kernel(xn_ref, wqt_ref, k1_ref, k2kh_ref, k2hk_ref, thr_ref, c_ref, s2_ref, e2_ref,
                 s1_scr, s2kh_scr, s2hk_scr, b0_scr):
    nlb = PREP_TT // BLK
    xn = xn_ref[...]
    qt = lax.dot_general(wqt_ref[...], xn, (((1,), (1,)), ((), ())), preferred_element_type=F32)
    nrow = PEER_HEADS * PEER_HALF
    q1 = qt[:nrow].astype(BF16)
    q2 = qt[nrow:].astype(BF16)
    s1 = jnp.dot(k1_ref[...], q1, preferred_element_type=F32)
    s2kh = jnp.dot(k2kh_ref[...], q2, preferred_element_type=F32)
    s2hk = jnp.dot(k2hk_ref[...], q2, preferred_element_type=F32)
    for lb in range(nlb):
        s1_scr[lb] = s1[:, lb * BLK:(lb + 1) * BLK]
        s2kh_scr[lb] = s2kh[:, lb * BLK:(lb + 1) * BLK]
        s2hk_scr[lb] = s2hk[:, lb * BLK:(lb + 1) * BLK]

    def lane_block(lb, carry):
        s1_rows = [s1_scr[lb, k * 8:(k + 1) * 8, :] for k in range(PEER_NKEYS)]
        s2_rows = [s2kh_scr[lb, k * 8:(k + 1) * 8, :] for k in range(PEER_NKEYS)]
        a = _top_sorted(s1_rows, PEER_TOPK)
        b = _top_sorted(s2_rows, PEER_TOPK)
        a_next = _max_below(s1_rows, a[PEER_TOPK - 1])
        b_next = _max_below(s2_rows, b[PEER_TOPK - 1])
        cands = [a[i] + b[jx] for i in range(PEER_TOPK) for jx in range(PEER_TOPK)
                 if (i + 1) * (jx + 1) <= PEER_TOPK]
        top = _top_sorted(cands, PEER_TOPK)
        runner_up = _max_below(cands + [a_next + b[0], a[0] + b_next], top[PEER_TOPK - 1])
        tau = 0.5 * (top[PEER_TOPK - 1] + runner_up)
        z = jnp.ones_like(tau)
        for kk in range(1, PEER_TOPK):
            z = z + jnp.exp(top[kk] - top[0])
        inv_z = 1.0 / z
        tau_rel = tau - b[0]
        for k in range(PEER_NKEYS):
            s1k = s1_scr[lb, k * 8:(k + 1) * 8, :]
            thr_ref[lb, k * 8:(k + 1) * 8, :] = tau_rel - s1k
            c_ref[lb, k * 8:(k + 1) * 8, :] = jnp.exp(s1k - a[0]) * inv_z
        b0_scr[lb] = b[0]
        return carry

    lax.fori_loop(0, nlb, lane_block, 0)

    for lb in range(nlb):
        for hh in range(PEER_HEADS):
            rows = slice(hh * PEER_NKEYS, (hh + 1) * PEER_NKEYS)
            s2_rel = s2hk_scr[lb, rows, :] - b0_scr[lb, hh:hh + 1, :]
            s2_ref[lb, rows, :] = s2_rel
            e2_ref[lb, rows, :] = jnp.exp(s2_rel)


def _peer_prep(xn, wqt, k1kh, k2kh, k2hk):
    T, D = xn.shape
    nrow = PEER_HEADS * PEER_NKEYS
    nlb = PREP_TT // BLK
    out_spec = pl.BlockSpec((nlb, nrow, BLK), lambda i: (i, 0, 0))
    rows_f32 = jax.ShapeDtypeStruct((T // BLK, nrow, BLK), F32)
    return pl.pallas_call(
        _prep_kernel,
        grid=(T // PREP_TT,),
        in_specs=[pl.BlockSpec((PREP_TT, D), lambda i: (i, 0)),
                  _const_spec(wqt.shape), _const_spec(k1kh.shape), _const_spec(k2kh.shape),
                  _const_spec(k2hk.shape)],
        out_specs=[out_spec] * 4,
        out_shape=[rows_f32] * 4,
        scratch_shapes=[pltpu.VMEM((nlb, nrow, BLK), F32), pltpu.VMEM((nlb, nrow, BLK), F32),
                        pltpu.VMEM((nlb, nrow, BLK), F32), pltpu.VMEM((nlb, PEER_HEADS, BLK), F32)],
        compiler_params=pltpu.CompilerParams(
            dimension_semantics=("arbitrary",), vmem_limit_bytes=VMEM_LIMIT),
        name="peer_prep",
    )(xn, wqt, k1kh, k2kh, k2hk)


def _peer_stage(c_gate, xn_ref, thr_ref, c_ref, s2_ref, e2_ref, u_ref, vt_ref, yt_scr,
                ht_w, ht_r, at_w, at_r, *, first_matmul=True, gate=True, second_matmul=True):
    n_i1 = PEER_EC // PEER_NKEYS
    nlb = PEER_TT // BLK
    if second_matmul:
        yt_scr[...] += jnp.dot(vt_ref[0], at_r[...], preferred_element_type=F32)
    if first_matmul:
        ht_w[...] = lax.dot_general(u_ref[...], xn_ref[...], (((1,), (1,)), ((), ())),
                                    preferred_element_type=F32)
    if not gate:
        return
    row0 = c_gate * (n_i1 * PEER_HEADS)
    for lb in range(nlb):
        lanes = slice(lb * BLK, (lb + 1) * BLK)
        for rb in range(PEER_NKEYS // GATE_ROWS):
            gates = [jnp.zeros((GATE_ROWS, BLK), F32) for _ in range(n_i1)]
            for hh in range(PEER_HEADS):
                rows = slice(hh * PEER_NKEYS + rb * GATE_ROWS, hh * PEER_NKEYS + (rb + 1) * GATE_ROWS)
                s2 = s2_ref[lb, rows, :]
                e2 = e2_ref[lb, rows, :]
                for ii in range(n_i1):
                    r = row0 + ii * PEER_HEADS + hh
                    thr = thr_ref[lb, pl.ds(r, 1), :]
                    cc = c_ref[lb, pl.ds(r, 1), :]
                    gates[ii] = gates[ii] + jnp.where(s2 >= thr, e2, 0.0) * cc
            for ii in range(n_i1):
                rows = slice(ii * PEER_NKEYS + rb * GATE_ROWS, ii * PEER_NKEYS + (rb + 1) * GATE_ROWS)
                at_w[rows, lanes] = (jax.nn.gelu(ht_r[rows, lanes]) * gates[ii]).astype(BF16)


def _peer_kernel(xn_ref, thr_ref, c_ref, s2_ref, e2_ref, u_ref, vt_ref, x1_ref, fw_ref, out_ref,
                 yt_scr, ht_a, ht_b, at_a, at_b):
    s = pl.program_id(1)
    n_chunks = PEER_N_EXPERTS // PEER_EC

    @pl.when(s == 0)
    def _():
        yt_scr[...] = jnp.zeros_like(yt_scr)

    c_gate = jnp.clip(s - 1, 0, n_chunks - 1)
    stage = functools.partial(_peer_stage, c_gate, xn_ref, thr_ref, c_ref, s2_ref, e2_ref, u_ref,
                              vt_ref, yt_scr)
    even = lax.rem(s, 2) == 0
    inner = jnp.logical_and(s >= 2, s < n_chunks)

    @pl.when(jnp.logical_and(even, inner))
    def _():
        stage(ht_a, ht_b, at_b, at_a)

    @pl.when(jnp.logical_and(jnp.logical_not(even), inner))
    def _():
        stage(ht_b, ht_a, at_a, at_b)

    @pl.when(s == 0)
    def _():
        stage(ht_a, ht_b, at_b, at_a, gate=False, second_matmul=False)

    @pl.when(s == 1)
    def _():
        stage(ht_b, ht_a, at_a, at_b, second_matmul=False)

    @pl.when(s == n_chunks)
    def _():
        stage(ht_a, ht_b, at_b, at_a, first_matmul=False)

    @pl.when(s == n_chunks + 1)
    def _():
        stage(ht_b, ht_a, at_a, at_b, first_matmul=False, gate=False)

    @pl.when(s == pl.num_programs(1) - 1)
    def _():
        xo = x1_ref[...] + yt_scr[...].T
        out_ref[...] = _rms(xo, fw_ref[...])


def _peer(xn, thr, cc, s2, e2, u, vt, x1, fw):
    T, D = xn.shape
    nrow = PEER_HEADS * PEER_NKEYS
    nlb = PEER_TT // BLK
    n_chunks = PEER_N_EXPERTS // PEER_EC
    assert n_chunks % 2 == 0 and n_chunks >= 4
    prep_spec = pl.BlockSpec((nlb, nrow, BLK), lambda i, s: (i, 0, 0))
    tok_spec = pl.BlockSpec((PEER_TT, D), lambda i, s: (i, 0))
    return pl.pallas_call(
        _peer_kernel,
        grid=(T // PEER_TT, n_chunks + 2),
        in_specs=[tok_spec, prep_spec, prep_spec, prep_spec, prep_spec,
                  pl.BlockSpec((PEER_EC, D), lambda i, s: (jnp.minimum(s, n_chunks - 1), 0)),
                  pl.BlockSpec((1, D, PEER_EC), lambda i, s: (jnp.clip(s - 2, 0, n_chunks - 1), 0, 0)),
                  tok_spec, _const_spec(fw.shape)],
        out_specs=tok_spec,
        out_shape=jax.ShapeDtypeStruct((T, D), F32),
        scratch_shapes=[pltpu.VMEM((D, PEER_TT), F32),
                        pltpu.VMEM((PEER_EC, PEER_TT), F32), pltpu.VMEM((PEER_EC, PEER_TT), F32),
                        pltpu.VMEM((PEER_EC, PEER_TT), BF16), pltpu.VMEM((PEER_EC, PEER_TT), BF16)],
        compiler_params=pltpu.CompilerParams(
            dimension_semantics=("arbitrary", "arbitrary"), vmem_limit_bytes=VMEM_LIMIT),
        name="peer_ffn",
    )(xn, thr, cc, s2, e2, u, vt, x1, fw)


def _t5_bucket(dist):
    n = np.maximum(dist, 0)
    max_exact = N_BUCKETS // 2
    large = max_exact + (np.log(np.maximum(n, 1) / max_exact) / np.log(MAX_DISTANCE / max_exact)
                         * (N_BUCKETS - max_exact)).astype(np.int32)
    large = np.minimum(large, N_BUCKETS - 1)
    return np.where(n < max_exact, n, large).astype(np.int32)


_PAIR_HEADS = [h for jj in range(Q_PER_KV) for h in (jj, Q_PER_KV + jj)]


def _band_bias(rel_bias):
    qi = np.arange(BLK)[:, None]
    kj = np.arange(2 * BLK)[None, :]
    dist = qi + BLK - kj
    in_window = (dist >= 0) & (dist < WINDOW)
    onehot = (_t5_bucket(dist)[:, :, None] == np.arange(N_BUCKETS)).astype(np.float32)
    per_head = rel_bias.astype(F32)[:, np.array(_PAIR_HEADS)]
    bias = jnp.einsum('qkb,bh->hqk', onehot, per_head, precision=lax.Precision.HIGHEST)
    bias = jnp.where(in_window[None], bias, NEG_INF)
    return bias.reshape(N_Q_HEADS * BLK, 2 * BLK)


def kernel(x, mix_norm_w, w_in, rel_bias, attn_sinks, w_branch_attn, gmlp_ln_w, gmlp_ln_b, gmlp_w_s,
           gmlp_b_s, w_branch_gmlp, w_out, ffn_norm_w, peer_w_q, peer_keys_1, peer_keys_2, peer_u,
           peer_v, final_norm_w):
    B, S, D = x.shape
    assert w_in.shape[0] == 1, "single-layer block"
    l = 0
    bias = _band_bias(rel_bias)
    causal = np.tril(np.ones((BLK, BLK), dtype=bool))
    eye = jnp.eye(PEER_HEADS, dtype=F32)
    nrow = PEER_HEADS * PEER_NKEYS

    w_q_pairs = (w_in[l][:, :ATTN_WIDTH].reshape(D, N_KV_HEADS, Q_PER_KV, HEAD_DIM)
                 .transpose(0, 2, 1, 3).reshape(D, ATTN_WIDTH))
    win = jnp.concatenate([w_q_pairs, w_in[l][:, ATTN_WIDTH:]], axis=1).astype(BF16)
    wba = (w_branch_attn[l].reshape(N_KV_HEADS, Q_PER_KV, HEAD_DIM, D)
           .transpose(1, 0, 2, 3).reshape(ATTN_WIDTH, D).astype(BF16))
    sinks = attn_sinks[l].astype(F32).reshape(N_KV_HEADS, Q_PER_KV).T.reshape(N_Q_HEADS)
    ws = jnp.where(causal[None], gmlp_w_s[l], 0).astype(BF16)
    btab = jnp.repeat(gmlp_b_s[l].astype(F32).T, GMLP_WIDTH // GMLP_GROUPS, axis=1)
    x1, xn = _mixer(x, sinks, mix_norm_w[l].reshape(1, D), win, bias, wba,
                    gmlp_ln_w[l].reshape(1, -1), gmlp_ln_b[l].reshape(1, -1), ws, btab,
                    w_branch_gmlp[l].astype(BF16), w_out[l].astype(BF16), ffn_norm_w[l].reshape(1, D))

    wqt = peer_w_q[l].reshape(D, PEER_HEADS, 2, PEER_HALF).transpose(2, 1, 3, 0).reshape(2 * nrow, D)
    k1kh = jnp.einsum('kd,hg->khgd', peer_keys_1[l], eye).reshape(nrow, nrow).astype(BF16)
    k2kh = jnp.einsum('kd,hg->khgd', peer_keys_2[l], eye).reshape(nrow, nrow).astype(BF16)
    k2hk = jnp.einsum('kd,hg->hkgd', peer_keys_2[l], eye).reshape(nrow, nrow).astype(BF16)
    xn2 = xn.reshape(B * S, D)
    thr, cc, s2, e2 = _peer_prep(xn2, wqt.astype(BF16), k1kh, k2kh, k2hk)
    vt = peer_v[l].reshape(PEER_N_EXPERTS // PEER_EC, PEER_EC, D).transpose(0, 2, 1).astype(BF16)
    y = _peer(xn2, thr, cc, s2, e2, peer_u[l].astype(BF16), vt,
              x1.reshape(B * S, D), final_norm_w.reshape(1, D))
    return y.reshape(B, S, D)
```

```python
import functools

import numpy as np
import jax
import jax.numpy as jnp
from jax import lax
from jax.experimental import pallas as pl
from jax.experimental.pallas import tpu as pltpu

F32 = jnp.float32
BF16 = jnp.bfloat16

D_MODEL = 1024
HEAD_DIM = 64
N_Q_HEADS = 8
N_KV_HEADS = 2
Q_PER_KV = N_Q_HEADS // N_KV_HEADS
ATTN_WIDTH = N_Q_HEADS * HEAD_DIM
KV_WIDTH = N_KV_HEADS * HEAD_DIM
WINDOW = 128
BLK = 128
N_BUCKETS = 32
MAX_DISTANCE = 128
GMLP_GROUPS = 8
GMLP_WIDTH = 512
PEER_HEADS = 8
PEER_NKEYS = 128
PEER_N_EXPERTS = PEER_NKEYS * PEER_NKEYS
PEER_HALF = 128
PEER_TOPK = 16
EPS = 1e-6
NEG_INF = -1e30
_GELU_C0 = float(np.sqrt(2.0 / np.pi))
_GELU_C1 = 0.044715 * _GELU_C0

_C_Q, _C_K, _C_V, _C_GU, _C_GV, _C_GA, _C_GB, _C_END = 0, 512, 640, 768, 1280, 1792, 2816, 3840

MIX_TM = 512
PREP_TT = 512
PEER_TT = 512
PEER_EC = 1024
GATE_ROWS = 32
VMEM_LIMIT = 56 * 1024 * 1024


def _rms(x, w):
    return x * lax.rsqrt(jnp.mean(x * x, axis=-1, keepdims=True) + EPS) * w


def _mixer_kernel(sink_ref, x_ref, nw_ref, win_ref, bias_ref, wba_ref, lnw_ref, lnb_ref, ws_ref,
                  btab_ref, wbg_ref, wout_ref, fnw_ref, x1_ref, xn_ref,
                  q_scr, kext, vext, gu_scr, gv_scr, ya_scr, yb_scr):
    j = pl.program_id(1)
    nb = MIX_TM // BLK
    x = x_ref[0]
    h = _rms(x, nw_ref[...]).astype(BF16)

    def proj(lo, hi):
        return jnp.dot(h, win_ref[:, lo:hi], preferred_element_type=F32)

    q_scr[...] = proj(_C_Q, _C_K) * (HEAD_DIM ** -0.5)

    @pl.when(j == 0)
    def _():
        kext[0:BLK, :] = jnp.zeros((BLK, KV_WIDTH), BF16)
        vext[0:BLK, :] = jnp.zeros((BLK, KV_WIDTH), BF16)

    kext[BLK:, :] = proj(_C_K, _C_V).astype(BF16)
    vext[BLK:, :] = proj(_C_V, _C_GU).astype(BF16)
    gu_scr[...] = jax.nn.gelu(proj(_C_GU, _C_GV))
    gv = jax.nn.gelu(proj(_C_GV, _C_GA))
    mu = jnp.mean(gv, axis=-1, keepdims=True)
    gc = gv - mu
    gvn = gc * lax.rsqrt(jnp.mean(gc * gc, axis=-1, keepdims=True) + EPS)
    gv_scr[...] = (gvn * lnw_ref[...] + lnb_ref[...]).astype(BF16)

    lane = lax.broadcasted_iota(jnp.int32, (BLK, BLK), 1)
    low = lane < HEAD_DIM
    col = lax.broadcasted_iota(jnp.int32, (BLK, 2 * BLK), 1)

    def block(b, carry):
        r0 = pl.multiple_of(b * BLK, BLK)
        qb = q_scr[pl.ds(r0, BLK), :]
        kb = kext[pl.ds(r0, 2 * BLK), :]
        vb = vext[pl.ds(r0, 2 * BLK), :]
        parts = []
        for jj in range(4):
            p_j = qb[:, jj * BLK:(jj + 1) * BLK]
            parts.append(jnp.where(low, p_j, 0.0).astype(BF16))
            parts.append(jnp.where(low, 0.0, p_j).astype(BF16))
        qs = jnp.concatenate(parts, axis=0)
        logits = lax.dot_general(qs, kb, (((1,), (1,)), ((), ())), preferred_element_type=F32)
        left_cut = jnp.where(j * nb + b == 0, BLK, 0)
        outs = []
        for r in range(8):
            lg = logits[r * BLK:(r + 1) * BLK, :] + bias_ref[r * BLK:(r + 1) * BLK, :]
            lg = jnp.where(col < left_cut, NEG_INF, lg)
            s = sink_ref[r]
            m = jnp.maximum(jnp.max(lg, axis=-1, keepdims=True), s)
            p = jnp.exp(lg - m)
            den = jnp.sum(p, axis=-1, keepdims=True) + jnp.exp(s - m)
            pv = jnp.dot(p.astype(BF16), vb, preferred_element_type=F32)
            outs.append(pv / den)
        pairs = [jnp.where(low, outs[2 * jj], outs[2 * jj + 1]) for jj in range(4)]
        ya_scr[pl.ds(r0, BLK), :] = jnp.concatenate(pairs, axis=1).astype(BF16)

        gub = gu_scr[pl.ds(r0, BLK), :]
        vnb = gv_scr[pl.ds(r0, BLK), :]
        ys = []
        for mm in range(4):
            vblk = vnb[:, mm * BLK:(mm + 1) * BLK]
            s0 = jnp.dot(ws_ref[2 * mm], vblk, preferred_element_type=F32)
            s1 = jnp.dot(ws_ref[2 * mm + 1], vblk, preferred_element_type=F32)
            sg = jnp.where(low, s0, s1) + btab_ref[:, mm * BLK:(mm + 1) * BLK]
            ys.append(gub[:, mm * BLK:(mm + 1) * BLK] * sg)
        yb_scr[pl.ds(r0, BLK), :] = jnp.concatenate(ys, axis=1).astype(BF16)
        return carry

    lax.fori_loop(0, nb, block, 0)

    kext[0:BLK, :] = kext[MIX_TM:MIX_TM + BLK, :]
    vext[0:BLK, :] = vext[MIX_TM:MIX_TM + BLK, :]

    ya_p = jnp.dot(ya_scr[...], wba_ref[...], preferred_element_type=F32)
    yb_p = jnp.dot(yb_scr[...], wbg_ref[...], preferred_element_type=F32)
    merged = jax.nn.sigmoid(proj(_C_GA, _C_GB)) * ya_p + jax.nn.sigmoid(proj(_C_GB, _C_END)) * yb_p
    x1 = x + jnp.dot(merged.astype(BF16), wout_ref[...], preferred_element_type=F32)
    x1_ref[0] = x1
    xn_ref[0] = _rms(x1, fnw_ref[...]).astype(BF16)


def _const_spec(shape):
    nd = len(shape)
    return pl.BlockSpec(shape, lambda *_: (0,) * nd)


def _mixer(x, sinks, nw, win, bias, wba, lnw, lnb, ws, btab, wbg, wout, fnw):
    B, S, D = x.shape
    grid = (B, S // MIX_TM)
    tile = pl.BlockSpec((1, MIX_TM, D), lambda b, j: (b, j, 0))
    in_specs = [
        pl.BlockSpec(memory_space=pltpu.SMEM),
        tile,
        _const_spec(nw.shape), _const_spec(win.shape), _const_spec(bias.shape), _const_spec(wba.shape),
        _const_spec(lnw.shape), _const_spec(lnb.shape), _const_spec(ws.shape), _const_spec(btab.shape),
        _const_spec(wbg.shape), _const_spec(wout.shape), _const_spec(fnw.shape),
    ]
    return pl.pallas_call(
        _mixer_kernel,
        grid=grid,
        in_specs=in_specs,
        out_specs=[tile, tile],
        out_shape=[jax.ShapeDtypeStruct((B, S, D), F32), jax.ShapeDtypeStruct((B, S, D), BF16)],
        scratch_shapes=[
            pltpu.VMEM((MIX_TM, ATTN_WIDTH), F32),
            pltpu.VMEM((MIX_TM + BLK, KV_WIDTH), BF16),
            pltpu.VMEM((MIX_TM + BLK, KV_WIDTH), BF16),
            pltpu.VMEM((MIX_TM, GMLP_WIDTH), F32),
            pltpu.VMEM((MIX_TM, GMLP_WIDTH), BF16),
            pltpu.VMEM((MIX_TM, ATTN_WIDTH), BF16),
            pltpu.VMEM((MIX_TM, GMLP_WIDTH), BF16),
        ],
        compiler_params=pltpu.CompilerParams(
            dimension_semantics=("arbitrary", "arbitrary"), vmem_limit_bytes=VMEM_LIMIT),
        name="mixer",
    )(sinks, x, nw, win, bias, wba, lnw, lnb, ws, btab, wbg, wout, fnw)


def _cmpx(a, b):
    if a is None:
        return b, None
    if b is None:
        return a, None
    return jnp.maximum(a, b), jnp.minimum(a, b)


def _bitonic_sort_desc(v):
    v = list(v)
    n = len(v)
    k = 2
    while k <= n:
        jj = k // 2
        while jj >= 1:
            for i in range(n):
                l = i ^ jj
                if l > i:
                    hi, lo = _cmpx(v[i], v[l])
                    if (i & k) == 0:
                        v[i], v[l] = hi, lo
                    else:
                        v[i], v[l] = lo, hi
            jj //= 2
        k *= 2
    return v


def _merge_top(a, b):
    n = len(a)
    c = [_cmpx(a[i], b[n - 1 - i])[0] for i in range(n)]
    jj = n // 2
    while jj >= 1:
        for i in range(n):
            if (i & jj) == 0:
                c[i], c[i + jj] = _cmpx(c[i], c[i + jj])
        jj //= 2
    return c


def _top_sorted(vals, k):
    vals = list(vals)
    while len(vals) % k:
        vals.append(None)
    groups = [_bitonic_sort_desc(vals[g:g + k]) for g in range(0, len(vals), k)]
    while len(groups) > 1:
        nxt = [_merge_top(groups[g], groups[g + 1]) for g in range(0, len(groups) - 1, 2)]
        if len(groups) % 2:
            nxt.append(groups[-1])
        groups = nxt
    return groups[0]


def _max_below(vals, bound):
    m = None
    for v in vals:
        w = jnp.where(v < bound, v, -3.0e38)
        m = w if m is None else jnp.maximum(m, w)
    return m


def _prep_kernel(xn_ref, wqt_ref, k1_ref, k2kh_ref, k2hk_ref, thr_ref, c_ref, s2_ref, e2_ref,
                 s1_scr, s2kh_scr, s2hk_scr, b0_scr):
    nlb = PREP_TT // BLK
    xn = xn_ref[...]
    qt = lax.dot_general(wqt_ref[...], xn, (((1,), (1,)), ((), ())), preferred_element_type=F32)
    nrow = PEER_HEADS * PEER_HALF
    q1 = qt[:nrow].astype(BF16)
    q2 = qt[nrow:].astype(BF16)
    s1 = jnp.dot(k1_ref[...], q1, preferred_element_type=F32)
    s2kh = jnp.dot(k2kh_ref[...], q2, preferred_element_type=F32)
    s2hk = jnp.dot(k2hk_ref[...], q2, preferred_element_type=F32)
    for lb in range(nlb):
        s1_scr[lb] = s1[:, lb * BLK:(lb + 1) * BLK]
        s2kh_scr[lb] = s2kh[:, lb * BLK:(lb + 1) * BLK]
        s2hk_scr[lb] = s2hk[:, lb * BLK:(lb + 1) * BLK]

    def lane_block(lb, carry):
        s1_rows = [s1_scr[lb, k * 8:(k + 1) * 8, :] for k in range(PEER_NKEYS)]
        s2_rows = [s2kh_scr[lb, k * 8:(k + 1) * 8, :] for k in range(PEER_NKEYS)]
        a = _top_sorted(s1_rows, PEER_TOPK)
        b = _top_sorted(s2_rows, PEER_TOPK)
        a_next = _max_below(s1_rows, a[PEER_TOPK - 1])
        b_next = _max_below(s2_rows, b[PEER_TOPK - 1])
        cands = [a[i] + b[jx] for i in range(PEER_TOPK) for jx in range(PEER_TOPK)
                 if (i + 1) * (jx + 1) <= PEER_TOPK]
        top = _top_sorted(cands, PEER_TOPK)
        runner_up = _max_below(cands + [a_next + b[0], a[0] + b_next], top[PEER_TOPK - 1])
        tau = 0.5 * (top[PEER_TOPK - 1] + runner_up)
        z = jnp.ones_like(tau)
        for kk in range(1, PEER_TOPK):
            z = z + jnp.exp(top[kk] - top[0])
        inv_z = 0.5 / z
        tau_rel = tau - b[0]
        for k in range(PEER_NKEYS):
            s1k = s1_scr[lb, k * 8:(k + 1) * 8, :]
            thr_ref[lb, k * 8:(k + 1) * 8, :] = tau_rel - s1k
            c_ref[lb, k * 8:(k + 1) * 8, :] = jnp.exp(s1k - a[0]) * inv_z
        b0_scr[lb] = b[0]
        return carry

    lax.fori_loop(0, nlb, lane_block, 0)

    for lb in range(nlb):
        for hh in range(PEER_HEADS):
            rows = slice(hh * PEER_NKEYS, (hh + 1) * PEER_NKEYS)
            s2_rel = s2hk_scr[lb, rows, :] - b0_scr[lb, hh:hh + 1, :]
            s2_ref[lb, rows, :] = s2_rel
            e2_ref[lb, rows, :] = jnp.exp(s2_rel)


def _peer_prep(xn, wqt, k1kh, k2kh, k2hk):
    T, D = xn.shape
    nrow = PEER_HEADS * PEER_NKEYS
    nlb = PREP_TT // BLK
    out_spec = pl.BlockSpec((nlb, nrow, BLK), lambda i: (i, 0, 0))
    rows_f32 = jax.ShapeDtypeStruct((T // BLK, nrow, BLK), F32)
    return pl.pallas_call(
        _prep_kernel,
        grid=(T // PREP_TT,),
        in_specs=[pl.BlockSpec((PREP_TT, D), lambda i: (i, 0)),
                  _const_spec(wqt.shape), _const_spec(k1kh.shape), _const_spec(k2kh.shape),
                  _const_spec(k2hk.shape)],
        out_specs=[out_spec] * 4,
        out_shape=[rows_f32] * 4,
        scratch_shapes=[pltpu.VMEM((nlb, nrow, BLK), F32), pltpu.VMEM((nlb, nrow, BLK), F32),
                        pltpu.VMEM((nlb, nrow, BLK), F32), pltpu.VMEM((nlb, PEER_HEADS, BLK), F32)],
        compiler_params=pltpu.CompilerParams(
            dimension_semantics=("arbitrary",), vmem_limit_bytes=VMEM_LIMIT),
        name="peer_prep",
    )(xn, wqt, k1kh, k2kh, k2hk)


def _peer_stage(c_gate, xn_ref, thr_ref, c_ref, s2_ref, e2_ref, u_ref, vt_ref, yt_scr,
                ht_w, ht_r, at_w, at_r, *, first_matmul=True, gate=True, second_matmul=True):
    n_i1 = PEER_EC // PEER_NKEYS
    nlb = PEER_TT // BLK
    if second_matmul:
        yt_scr[...] += jnp.dot(vt_ref[0], at_r[...], preferred_element_type=F32)
    if first_matmul:
        ht_w[...] = lax.dot_general(u_ref[...], xn_ref[...], (((1,), (1,)), ((), ())),
                                    preferred_element_type=F32)
    if not gate:
        return
    row0 = c_gate * (n_i1 * PEER_HEADS)
    for lb in range(nlb):
        lanes = slice(lb * BLK, (lb + 1) * BLK)
        for rb in range(PEER_NKEYS // GATE_ROWS):
            gates = [jnp.zeros((GATE_ROWS, BLK), F32) for _ in range(n_i1)]
            for hh in range(PEER_HEADS):
                rows = slice(hh * PEER_NKEYS + rb * GATE_ROWS, hh * PEER_NKEYS + (rb + 1) * GATE_ROWS)
                s2 = s2_ref[lb, rows, :]
                e2 = e2_ref[lb, rows, :]
                for ii in range(n_i1):
                    r = row0 + ii * PEER_HEADS + hh
                    thr = thr_ref[lb, pl.ds(r, 1), :]
                    cc = c_ref[lb, pl.ds(r, 1), :]
                    gates[ii] = gates[ii] + jnp.where(s2 >= thr, e2, 0.0) * cc
            for ii in range(n_i1):
                rows = slice(ii * PEER_NKEYS + rb * GATE_ROWS, ii * PEER_NKEYS + (rb + 1) * GATE_ROWS)
                x = ht_r[rows, lanes]
                t = jnp.tanh(x * (_GELU_C0 + _GELU_C1 * (x * x)))
                at_w[rows, lanes] = ((x + x * t) * gates[ii]).astype(BF16)


def _peer_kernel(xn_ref, thr_ref, c_ref, s2_ref, e2_ref, u_ref, vt_ref, x1_ref, fw_ref, out_ref,
                 yt_scr, ht_a, ht_b, at_a, at_b):
    s = pl.program_id(1)
    n_chunks = PEER_N_EXPERTS // PEER_EC

    @pl.when(s == 0)
    def _():
        yt_scr[...] = jnp.zeros_like(yt_scr)

    c_gate = jnp.clip(s - 1, 0, n_chunks - 1)
    stage = functools.partial(_peer_stage, c_gate, xn_ref, thr_ref, c_ref, s2_ref, e2_ref, u_ref,
                              vt_ref, yt_scr)
    even = lax.rem(s, 2) == 0
    inner = jnp.logical_and(s >= 2, s < n_chunks)

    @pl.when(jnp.logical_and(even, inner))
    def _():
        stage(ht_a, ht_b, at_b, at_a)

    @pl.when(jnp.logical_and(jnp.logical_not(even), inner))
    def _():
        stage(ht_b, ht_a, at_a, at_b)

    @pl.when(s == 0)
    def _():
        stage(ht_a, ht_b, at_b, at_a, gate=False, second_matmul=False)

    @pl.when(s == 1)
    def _():
        stage(ht_b, ht_a, at_a, at_b, second_matmul=False)

    @pl.when(s == n_chunks)
    def _():
        stage(ht_a, ht_b, at_b, at_a, first_matmul=False)

    @pl.when(s == n_chunks + 1)
    def _():
        stage(ht_b, ht_a, at_a, at_b, first_matmul=False, gate=False)

    @pl.when(s == pl.num_programs(1) - 1)
    def _():
        xo = x1_ref[...] + yt_scr[...].T
        out_ref[...] = _rms(xo, fw_ref[...])


def _peer(xn, thr, cc, s2, e2, u, vt, x1, fw):
    T, D = xn.shape
    nrow = PEER_HEADS * PEER_NKEYS
    nlb = PEER_TT // BLK
    n_chunks = PEER_N_EXPERTS // PEER_EC
    assert n_chunks % 2 == 0 and n_chunks >= 4
    prep_spec = pl.BlockSpec((nlb, nrow, BLK), lambda i, s: (i, 0, 0))
    tok_spec = pl.BlockSpec((PEER_TT, D), lambda i, s: (i, 0))
    return pl.pallas_call(
        _peer_kernel,
        grid=(T // PEER_TT, n_chunks + 2),
        in_specs=[tok_spec, prep_spec, prep_spec, prep_spec, prep_spec,
                  pl.BlockSpec((PEER_EC, D), lambda i, s: (jnp.minimum(s, n_chunks - 1), 0)),
                  pl.BlockSpec((1, D, PEER_EC), lambda i, s: (jnp.clip(s - 2, 0, n_chunks - 1), 0, 0)),
                  tok_spec, _const_spec(fw.shape)],
        out_specs=tok_spec,
        out_shape=jax.ShapeDtypeStruct((T, D), F32),
        scratch_shapes=[pltpu.VMEM((D, PEER_TT), F32),
                        pltpu.VMEM((PEER_EC, PEER_TT), F32), pltpu.VMEM((PEER_EC, PEER_TT), F32),
                        pltpu.VMEM((PEER_EC, PEER_TT), BF16), pltpu.VMEM((PEER_EC, PEER_TT), BF16)],
        compiler_params=pltpu.CompilerParams(
            dimension_semantics=("arbitrary", "arbitrary"), vmem_limit_bytes=VMEM_LIMIT),
        name="peer_ffn",
    )(xn, thr, cc, s2, e2, u, vt, x1, fw)


def _t5_bucket(dist):
    n = np.maximum(dist, 0)
    max_exact = N_BUCKETS // 2
    large = max_exact + (np.log(np.maximum(n, 1) / max_exact) / np.log(MAX_DISTANCE / max_exact)
                         * (N_BUCKETS - max_exact)).astype(np.int32)
    large = np.minimum(large, N_BUCKETS - 1)
    return np.where(n < max_exact, n, large).astype(np.int32)


_PAIR_HEADS = [h for jj in range(Q_PER_KV) for h in (jj, Q_PER_KV + jj)]


def _band_bias(rel_bias):
    qi = np.arange(BLK)[:, None]
    kj = np.arange(2 * BLK)[None, :]
    dist = qi + BLK - kj
    in_window = (dist >= 0) & (dist < WINDOW)
    onehot = (_t5_bucket(dist)[:, :, None] == np.arange(N_BUCKETS)).astype(np.float32)
    per_head = rel_bias.astype(F32)[:, np.array(_PAIR_HEADS)]
    bias = jnp.einsum('qkb,bh->hqk', onehot, per_head, precision=lax.Precision.HIGHEST)
    bias = jnp.where(in_window[None], bias, NEG_INF)
    return bias.reshape(N_Q_HEADS * BLK, 2 * BLK)


def kernel(x, mix_norm_w, w_in, rel_bias, attn_sinks, w_branch_attn, gmlp_ln_w, gmlp_ln_b, gmlp_w_s,
           gmlp_b_s, w_branch_gmlp, w_out, ffn_norm_w, peer_w_q, peer_keys_1, peer_keys_2, peer_u,
           peer_v, final_norm_w):
    B, S, D = x.shape
    assert w_in.shape[0] == 1, "single-layer block"
    l = 0
    bias = _band_bias(rel_bias)
    causal = np.tril(np.ones((BLK, BLK), dtype=bool))
    eye = jnp.eye(PEER_HEADS, dtype=F32)
    nrow = PEER_HEADS * PEER_NKEYS

    w_q_pairs = (w_in[l][:, :ATTN_WIDTH].reshape(D, N_KV_HEADS, Q_PER_KV, HEAD_DIM)
                 .transpose(0, 2, 1, 3).reshape(D, ATTN_WIDTH))
    win = jnp.concatenate([w_q_pairs, w_in[l][:, ATTN_WIDTH:]], axis=1).astype(BF16)
    wba = (w_branch_attn[l].reshape(N_KV_HEADS, Q_PER_KV, HEAD_DIM, D)
           .transpose(1, 0, 2, 3).reshape(ATTN_WIDTH, D).astype(BF16))
    sinks = attn_sinks[l].astype(F32).reshape(N_KV_HEADS, Q_PER_KV).T.reshape(N_Q_HEADS)
    ws = jnp.where(causal[None], gmlp_w_s[l], 0).astype(BF16)
    btab = jnp.repeat(gmlp_b_s[l].astype(F32).T, GMLP_WIDTH // GMLP_GROUPS, axis=1)
    x1, xn = _mixer(x, sinks, mix_norm_w[l].reshape(1, D), win, bias, wba,
                    gmlp_ln_w[l].reshape(1, -1), gmlp_ln_b[l].reshape(1, -1), ws, btab,
                    w_branch_gmlp[l].astype(BF16), w_out[l].astype(BF16), ffn_norm_w[l].reshape(1, D))

    wqt = peer_w_q[l].reshape(D, PEER_HEADS, 2, PEER_HALF).transpose(2, 1, 3, 0).reshape(2 * nrow, D)
    k1kh = jnp.einsum('kd,hg->khgd', peer_keys_1[l], eye).reshape(nrow, nrow).astype(BF16)
    k2kh = jnp.einsum('kd,hg->khgd', peer_keys_2[l], eye).reshape(nrow, nrow).astype(BF16)
    k2hk = jnp.einsum('kd,hg->hkgd', peer_keys_2[l], eye).reshape(nrow, nrow).astype(BF16)
    xn2 = xn.reshape(B * S, D)
    thr, cc, s2, e2 = _peer_prep(xn2, wqt.astype(BF16), k1kh, k2kh, k2hk)
    vt = peer_v[l].reshape(PEER_N_EXPERTS // PEER_EC, PEER_EC, D).transpose(0, 2, 1).astype(BF16)
    y = _peer(xn2, thr, cc, s2, e2, peer_u[l], vt,
              x1.reshape(B * S, D), final_norm_w.reshape(1, D))
    return y.reshape(B, S, D)
```

```python
import functools

import numpy as np
import jax
import jax.numpy as jnp
from jax import lax
from jax.experimental import pallas as pl
from jax.experimental.pallas import tpu as pltpu

F32 = jnp.float32
BF16 = jnp.bfloat16

D_MODEL = 1024
HEAD_DIM = 64
N_Q_HEADS = 8
N_KV_HEADS = 2
Q_PER_KV = N_Q_HEADS // N_KV_HEADS
ATTN_WIDTH = N_Q_HEADS * HEAD_DIM
KV_WIDTH = N_KV_HEADS * HEAD_DIM
WINDOW = 128
BLK = 128
N_BUCKETS = 32
MAX_DISTANCE = 128
GMLP_GROUPS = 8
GMLP_WIDTH = 512
PEER_HEADS = 8
PEER_NKEYS = 128
PEER_N_EXPERTS = PEER_NKEYS * PEER_NKEYS
PEER_HALF = 128
PEER_TOPK = 16
EPS = 1e-6
NEG_INF = -1e30
_GELU_C0 = float(np.sqrt(2.0 / np.pi))
_GELU_C1 = 0.044715 * _GELU_C0

_C_Q, _C_K, _C_V, _C_GU, _C_GV, _C_GA, _C_GB, _C_END = 0, 512, 640, 768, 1280, 1792, 2816, 3840

MIX_TM = 512
PREP_TT = 512
PEER_TT = 512
PEER_EC = 1024
GATE_ROWS = 32
VMEM_LIMIT = 56 * 1024 * 1024


def _rms(x, w):
    return x * lax.rsqrt(jnp.mean(x * x, axis=-1, keepdims=True) + EPS) * w


def _mixer_kernel(sink_ref, x_ref, nw_ref, win_ref, bias_ref, wba_ref, lnw_ref, lnb_ref, ws_ref,
                  btab_ref, wbg_ref, wout_ref, fnw_ref, x1_ref, xn_ref,
                  q_scr, kext, vext, gu_scr, gv_scr, ya_scr, yb_scr):
    j = pl.program_id(1)
    nb = MIX_TM // BLK
    x = x_ref[0]
    h = _rms(x, nw_ref[...]).astype(BF16)

    def proj(lo, hi):
        return jnp.dot(h, win_ref[:, lo:hi], preferred_element_type=F32)

    q_scr[...] = proj(_C_Q, _C_K) * (HEAD_DIM ** -0.5)

    @pl.when(j == 0)
    def _():
        kext[0:BLK, :] = jnp.zeros((BLK, KV_WIDTH), BF16)
        vext[0:BLK, :] = jnp.zeros((BLK, KV_WIDTH), BF16)

    kext[BLK:, :] = proj(_C_K, _C_V).astype(BF16)
    vext[BLK:, :] = proj(_C_V, _C_GU).astype(BF16)
    gu_scr[...] = jax.nn.gelu(proj(_C_GU, _C_GV))
    gv = jax.nn.gelu(proj(_C_GV, _C_GA))
    mu = jnp.mean(gv, axis=-1, keepdims=True)
    gc = gv - mu
    gvn = gc * lax.rsqrt(jnp.mean(gc * gc, axis=-1, keepdims=True) + EPS)
    gv_scr[...] = (gvn * lnw_ref[...] + lnb_ref[...]).astype(BF16)

    lane = lax.broadcasted_iota(jnp.int32, (BLK, BLK), 1)
    low = lane < HEAD_DIM
    col = lax.broadcasted_iota(jnp.int32, (BLK, 2 * BLK), 1)

    def block(b, carry):
        r0 = pl.multiple_of(b * BLK, BLK)
        qb = q_scr[pl.ds(r0, BLK), :]
        kb = kext[pl.ds(r0, 2 * BLK), :]
        vb = vext[pl.ds(r0, 2 * BLK), :]
        parts = []
        for jj in range(4):
            p_j = qb[:, jj * BLK:(jj + 1) * BLK]
            parts.append(jnp.where(low, p_j, 0.0).astype(BF16))
            parts.append(jnp.where(low, 0.0, p_j).astype(BF16))
        qs = jnp.concatenate(parts, axis=0)
        logits = lax.dot_general(qs, kb, (((1,), (1,)), ((), ())), preferred_element_type=F32)
        left_cut = jnp.where(j * nb + b == 0, BLK, 0)
        outs = []
        for r in range(8):
            lg = logits[r * BLK:(r + 1) * BLK, :] + bias_ref[r * BLK:(r + 1) * BLK, :]
            lg = jnp.where(col < left_cut, NEG_INF, lg)
            s = sink_ref[r]
            m = jnp.maximum(jnp.max(lg, axis=-1, keepdims=True), s)
            p = jnp.exp(lg - m)
            den = jnp.sum(p, axis=-1, keepdims=True) + jnp.exp(s - m)
            pv = jnp.dot(p.astype(BF16), vb, preferred_element_type=F32)
            outs.append(pv / den)
        pairs = [jnp.where(low, outs[2 * jj], outs[2 * jj + 1]) for jj in range(4)]
        ya_scr[pl.ds(r0, BLK), :] = jnp.concatenate(pairs, axis=1).astype(BF16)

        gub = gu_scr[pl.ds(r0, BLK), :]
        vnb = gv_scr[pl.ds(r0, BLK), :]
        ys = []
        for mm in range(4):
            vblk = vnb[:, mm * BLK:(mm + 1) * BLK]
            s0 = jnp.dot(ws_ref[2 * mm], vblk, preferred_element_type=F32)
            s1 = jnp.dot(ws_ref[2 * mm + 1], vblk, preferred_element_type=F32)
            sg = jnp.where(low, s0, s1) + btab_ref[:, mm * BLK:(mm + 1) * BLK]
            ys.append(gub[:, mm * BLK:(mm + 1) * BLK] * sg)
        yb_scr[pl.ds(r0, BLK), :] = jnp.concatenate(ys, axis=1).astype(BF16)
        return carry

    lax.fori_loop(0, nb, block, 0)

    kext[0:BLK, :] = kext[MIX_TM:MIX_TM + BLK, :]
    vext[0:BLK, :] = vext[MIX_TM:MIX_TM + BLK, :]

    ya_p = jnp.dot(ya_scr[...], wba_ref[...], preferred_element_type=F32)
    yb_p = jnp.dot(yb_scr[...], wbg_ref[...], preferred_element_type=F32)
    merged = jax.nn.sigmoid(proj(_C_GA, _C_GB)) * ya_p + jax.nn.sigmoid(proj(_C_GB, _C_END)) * yb_p
    x1 = x + jnp.dot(merged.astype(BF16), wout_ref[...], preferred_element_type=F32)
    x1_ref[0] = x1
    xn_ref[0] = _rms(x1, fnw_ref[...]).astype(BF16)


def _const_spec(shape):
    nd = len(shape)
    return pl.BlockSpec(shape, lambda *_: (0,) * nd)


def _mixer(x, sinks, nw, win, bias, wba, lnw, lnb, ws, btab, wbg, wout, fnw):
    B, S, D = x.shape
    grid = (B, S // MIX_TM)
    tile = pl.BlockSpec((1, MIX_TM, D), lambda b, j: (b, j, 0))
    in_specs = [
        pl.BlockSpec(memory_space=pltpu.SMEM),
        tile,
        _const_spec(nw.shape), _const_spec(win.shape), _const_spec(bias.shape), _const_spec(wba.shape),
        _const_spec(lnw.shape), _const_spec(lnb.shape), _const_spec(ws.shape), _const_spec(btab.shape),
        _const_spec(wbg.shape), _const_spec(wout.shape), _const_spec(fnw.shape),
    ]
    return pl.pallas_call(
        _mixer_kernel,
        grid=grid,
        in_specs=in_specs,
        out_specs=[tile, tile],
        out_shape=[jax.ShapeDtypeStruct((B, S, D), F32), jax.ShapeDtypeStruct((B, S, D), BF16)],
        scratch_shapes=[
            pltpu.VMEM((MIX_TM, ATTN_WIDTH), F32),
            pltpu.VMEM((MIX_TM + BLK, KV_WIDTH), BF16),
            pltpu.VMEM((MIX_TM + BLK, KV_WIDTH), BF16),
            pltpu.VMEM((MIX_TM, GMLP_WIDTH), F32),
            pltpu.VMEM((MIX_TM, GMLP_WIDTH), BF16),
            pltpu.VMEM((MIX_TM, ATTN_WIDTH), BF16),
            pltpu.VMEM((MIX_TM, GMLP_WIDTH), BF16),
        ],
        compiler_params=pltpu.CompilerParams(
            dimension_semantics=("arbitrary", "arbitrary"), vmem_limit_bytes=VMEM_LIMIT),
        name="mixer",
    )(sinks, x, nw, win, bias, wba, lnw, lnb, ws, btab, wbg, wout, fnw)


def _cmpx(a, b):
    if a is None:
        return b, None
    if b is None:
        return a, None
    return jnp.maximum(a, b), jnp.minimum(a, b)


def _bitonic_sort_desc(v):
    v = list(v)
    n = len(v)
    k = 2
    while k <= n:
        jj = k // 2
        while jj >= 1:
            for i in range(n):
                l = i ^ jj
                if l > i:
                    hi, lo = _cmpx(v[i], v[l])
                    if (i & k) == 0:
                        v[i], v[l] = hi, lo
                    else:
                        v[i], v[l] = lo, hi
            jj //= 2
        k *= 2
    return v


def _merge_top(a, b):
    n = len(a)
    c = [_cmpx(a[i], b[n - 1 - i])[0] for i in range(n)]
    jj = n // 2
    while jj >= 1:
        for i in range(n):
            if (i & jj) == 0:
                c[i], c[i + jj] = _cmpx(c[i], c[i + jj])
        jj //= 2
    return c


def _top_sorted(vals, k):
    vals = list(vals)
    while len(vals) % k:
        vals.append(None)
    groups = [_bitonic_sort_desc(vals[g:g + k]) for g in range(0, len(vals), k)]
    while len(groups) > 1:
        nxt = [_merge_top(groups[g], groups[g + 1]) for g in range(0, len(groups) - 1, 2)]
        if len(groups) % 2:
            nxt.append(groups[-1])
        groups = nxt
    return groups[0]


def _max_below(vals, bound):
    m = None
    for v in vals:
        w = jnp.where(v < bound, v, -3.0e38)
        m = w if m is None else jnp.maximum(m, w)
    return m


def _prep_kernel(xn_ref, wqt_ref, k1_ref, k2kh_ref, k2hk_ref, thr_ref, c_ref, s2_ref, e2_ref,
                 s1_scr, s2kh_scr, s2hk_scr, b0_scr):
    nlb = PREP_TT // BLK
    xn = xn_ref[...]
    qt = lax.dot_general(wqt_ref[...], xn, (((1,), (1,)), ((), ())), preferred_element_type=F32)
    nrow = PEER_HEADS * PEER_HALF
    q1 = qt[:nrow].astype(BF16)
    q2 = qt[nrow:].astype(BF16)
    s1 = jnp.dot(k1_ref[...], q1, preferred_element_type=F32)
    s2kh = jnp.dot(k2kh_ref[...], q2, preferred_element_type=F32)
    s2hk = jnp.dot(k2hk_ref[...], q2, preferred_element_type=F32)
    for lb in range(nlb):
        s1_scr[lb] = s1[:, lb * BLK:(lb + 1) * BLK]
        s2kh_scr[lb] = s2kh[:, lb * BLK:(lb + 1) * BLK]
        s2hk_scr[lb] = s2hk[:, lb * BLK:(lb + 1) * BLK]

    def lane_block(lb, carry):
        s1_rows = [s1_scr[lb, k * 8:(k + 1) * 8, :] for k in range(PEER_NKEYS)]
        s2_rows = [s2kh_scr[lb, k * 8:(k + 1) * 8, :] for k in range(PEER_NKEYS)]
        a = _top_sorted(s1_rows, PEER_TOPK)
        b = _top_sorted(s2_rows, PEER_TOPK)
        a_next = _max_below(s1_rows, a[PEER_TOPK - 1])
        b_next = _max_below(s2_rows, b[PEER_TOPK - 1])
        cands = [a[i] + b[jx] for i in range(PEER_TOPK) for jx in range(PEER_TOPK)
                 if (i + 1) * (jx + 1) <= PEER_TOPK]
        top = _top_sorted(cands, PEER_TOPK)
        runner_up = _max_below(cands + [a_next + b[0], a[0] + b_next], top[PEER_TOPK - 1])
        tau = 0.5 * (top[PEER_TOPK - 1] + runner_up)
        z = jnp.ones_like(tau)
        for kk in range(1, PEER_TOPK):
            z = z + jnp.exp(top[kk] - top[0])
        inv_z = 0.5 / z
        tau_rel = tau - b[0]
        for k in range(PEER_NKEYS):
            s1k = s1_scr[lb, k * 8:(k + 1) * 8, :]
            thr_ref[lb, k * 8:(k + 1) * 8, :] = tau_rel - s1k
            c_ref[lb, k * 8:(k + 1) * 8, :] = jnp.exp(s1k - a[0]) * inv_z
        b0_scr[lb] = b[0]
        return carry

    lax.fori_loop(0, nlb, lane_block, 0)

    for lb in range(nlb):
        for hh in range(PEER_HEADS):
            rows = slice(hh * PEER_NKEYS, (hh + 1) * PEER_NKEYS)
            s2_rel = s2hk_scr[lb, rows, :] - b0_scr[lb, hh:hh + 1, :]
            s2_ref[lb, rows, :] = s2_rel
            e2_ref[lb, rows, :] = jnp.exp(s2_rel)


def _peer_prep(xn, wqt, k1kh, k2kh, k2hk):
    T, D = xn.shape
    nrow = PEER_HEADS * PEER_NKEYS
    nlb = PREP_TT // BLK
    out_spec = pl.BlockSpec((nlb, nrow, BLK), lambda i: (i, 0, 0))
    rows_f32 = jax.ShapeDtypeStruct((T // BLK, nrow, BLK), F32)
    return pl.pallas_call(
        _prep_kernel,
        grid=(T // PREP_TT,),
        in_specs=[pl.BlockSpec((PREP_TT, D), lambda i: (i, 0)),
                  _const_spec(wqt.shape), _const_spec(k1kh.shape), _const_spec(k2kh.shape),
                  _const_spec(k2hk.shape)],
        out_specs=[out_spec] * 4,
        out_shape=[rows_f32] * 4,
        scratch_shapes=[pltpu.VMEM((nlb, nrow, BLK), F32), pltpu.VMEM((nlb, nrow, BLK), F32),
                        pltpu.VMEM((nlb, nrow, BLK), F32), pltpu.VMEM((nlb, PEER_HEADS, BLK), F32)],
        compiler_params=pltpu.CompilerParams(
            dimension_semantics=("arbitrary",), vmem_limit_bytes=VMEM_LIMIT),
        name="peer_prep",
    )(xn, wqt, k1kh, k2kh, k2hk)


GROUP = 256


def _peer_stage(c_gate, xn_ref, thr_ref, c_ref, s2_ref, e2_ref, u_ref, vt_ref, yt_scr, at_scr,
                ht_w, ht_r, *, first_matmul=True, gate=True):
    n_i1 = GROUP // PEER_NKEYS
    if first_matmul:
        ht_w[...] = lax.dot_general(u_ref[...], xn_ref[...], (((1,), (1,)), ((), ())),
                                    preferred_element_type=F32)
    if not gate:
        return
    row0 = c_gate * ((PEER_EC // PEER_NKEYS) * PEER_HEADS)
    for nt in range(PEER_TT // GROUP):
        cols = slice(nt * GROUP, (nt + 1) * GROUP)
        for kt in range(PEER_EC // GROUP):
            krows = slice(kt * GROUP, (kt + 1) * GROUP)
            for lb in range(nt * (GROUP // BLK), (nt + 1) * (GROUP // BLK)):
                lanes = slice(lb * BLK, (lb + 1) * BLK)
                for rb in range(PEER_NKEYS // GATE_ROWS):
                    gates = [jnp.zeros((GATE_ROWS, BLK), F32) for _ in range(n_i1)]
                    for hh in range(PEER_HEADS):
                        rows = slice(hh * PEER_NKEYS + rb * GATE_ROWS, hh * PEER_NKEYS + (rb + 1) * GATE_ROWS)
                        s2 = s2_ref[lb, rows, :]
                        e2 = e2_ref[lb, rows, :]
                        for ii in range(n_i1):
                            r = row0 + (kt * n_i1 + ii) * PEER_HEADS + hh
                            thr = thr_ref[lb, pl.ds(r, 1), :]
                            cc = c_ref[lb, pl.ds(r, 1), :]
                            gates[ii] = gates[ii] + jnp.where(s2 >= thr, e2, 0.0) * cc
                    for ii in range(n_i1):
                        r0 = kt * GROUP + ii * PEER_NKEYS + rb * GATE_ROWS
                        x = ht_r[r0:r0 + GATE_ROWS, lanes]
                        t = jnp.tanh(x * (_GELU_C0 + _GELU_C1 * (x * x)))
                        at_scr[r0:r0 + GATE_ROWS, lanes] = ((x + x * t) * gates[ii]).astype(BF16)
            yt_scr[:, cols] += jnp.dot(vt_ref[0, :, krows], at_scr[krows, cols],
                                       preferred_element_type=F32)


def _peer_kernel(xn_ref, thr_ref, c_ref, s2_ref, e2_ref, u_ref, vt_ref, x1_ref, fw_ref, out_ref,
                 yt_scr, at_scr, ht_a, ht_b):
    s = pl.program_id(1)
    n_chunks = PEER_N_EXPERTS // PEER_EC

    @pl.when(s == 0)
    def _():
        yt_scr[...] = jnp.zeros_like(yt_scr)

    c_gate = jnp.clip(s - 1, 0, n_chunks - 1)
    stage = functools.partial(_peer_stage, c_gate, xn_ref, thr_ref, c_ref, s2_ref, e2_ref, u_ref,
                              vt_ref, yt_scr, at_scr)
    even = lax.rem(s, 2) == 0
    inner = jnp.logical_and(s >= 1, s < n_chunks)

    @pl.when(jnp.logical_and(even, inner))
    def _():
        stage(ht_a, ht_b)

    @pl.when(jnp.logical_and(jnp.logical_not(even), inner))
    def _():
        stage(ht_b, ht_a)

    @pl.when(s == 0)
    def _():
        stage(ht_a, ht_b, gate=False)

    @pl.when(s == n_chunks)
    def _():
        stage(ht_a, ht_b, first_matmul=False)
        xo = x1_ref[...] + yt_scr[...].T
        out_ref[...] = _rms(xo, fw_ref[...])


def _peer(xn, thr, cc, s2, e2, u, vt, x1, fw):
    T, D = xn.shape
    nrow = PEER_HEADS * PEER_NKEYS
    nlb = PEER_TT // BLK
    n_chunks = PEER_N_EXPERTS // PEER_EC
    assert n_chunks % 2 == 0
    prep_spec = pl.BlockSpec((nlb, nrow, BLK), lambda i, s: (i, 0, 0))
    tok_spec = pl.BlockSpec((PEER_TT, D), lambda i, s: (i, 0))
    return pl.pallas_call(
        _peer_kernel,
        grid=(T // PEER_TT, n_chunks + 1),
        in_specs=[tok_spec, prep_spec, prep_spec, prep_spec, prep_spec,
                  pl.BlockSpec((PEER_EC, D), lambda i, s: (jnp.minimum(s, n_chunks - 1), 0)),
                  pl.BlockSpec((1, D, PEER_EC), lambda i, s: (jnp.maximum(s - 1, 0), 0, 0)),
                  tok_spec, _const_spec(fw.shape)],
        out_specs=tok_spec,
        out_shape=jax.ShapeDtypeStruct((T, D), F32),
        scratch_shapes=[pltpu.VMEM((D, PEER_TT), F32), pltpu.VMEM((PEER_EC, PEER_TT), BF16),
                        pltpu.VMEM((PEER_EC, PEER_TT), F32), pltpu.VMEM((PEER_EC, PEER_TT), F32)],
        compiler_params=pltpu.CompilerParams(
            dimension_semantics=("arbitrary", "arbitrary"), vmem_limit_bytes=VMEM_LIMIT),
        name="peer_ffn",
    )(xn, thr, cc, s2, e2, u, vt, x1, fw)


def _t5_bucket(dist):
    n = np.maximum(dist, 0)
    max_exact = N_BUCKETS // 2
    large = max_exact + (np.log(np.maximum(n, 1) / max_exact) / np.log(MAX_DISTANCE / max_exact)
                         * (N_BUCKETS - max_exact)).astype(np.int32)
    large = np.minimum(large, N_BUCKETS - 1)
    return np.where(n < max_exact, n, large).astype(np.int32)


_PAIR_HEADS = [h for jj in range(Q_PER_KV) for h in (jj, Q_PER_KV + jj)]


def _band_bias(rel_bias):
    qi = np.arange(BLK)[:, None]
    kj = np.arange(2 * BLK)[None, :]
    dist = qi + BLK - kj
    in_window = (dist >= 0) & (dist < WINDOW)
    onehot = (_t5_bucket(dist)[:, :, None] == np.arange(N_BUCKETS)).astype(np.float32)
    per_head = rel_bias.astype(F32)[:, np.array(_PAIR_HEADS)]
    bias = jnp.einsum('qkb,bh->hqk', onehot, per_head, precision=lax.Precision.HIGHEST)
    bias = jnp.where(in_window[None], bias, NEG_INF)
    return bias.reshape(N_Q_HEADS * BLK, 2 * BLK)


def kernel(x, mix_norm_w, w_in, rel_bias, attn_sinks, w_branch_attn, gmlp_ln_w, gmlp_ln_b, gmlp_w_s,
           gmlp_b_s, w_branch_gmlp, w_out, ffn_norm_w, peer_w_q, peer_keys_1, peer_keys_2, peer_u,
           peer_v, final_norm_w):
    B, S, D = x.shape
    assert w_in.shape[0] == 1, "single-layer block"
    l = 0
    bias = _band_bias(rel_bias)
    causal = np.tril(np.ones((BLK, BLK), dtype=bool))
    eye = jnp.eye(PEER_HEADS, dtype=F32)
    nrow = PEER_HEADS * PEER_NKEYS

    w_q_pairs = (w_in[l][:, :ATTN_WIDTH].reshape(D, N_KV_HEADS, Q_PER_KV, HEAD_DIM)
                 .transpose(0, 2, 1, 3).reshape(D, ATTN_WIDTH))
    win = jnp.concatenate([w_q_pairs, w_in[l][:, ATTN_WIDTH:]], axis=1).astype(BF16)
    wba = (w_branch_attn[l].reshape(N_KV_HEADS, Q_PER_KV, HEAD_DIM, D)
           .transpose(1, 0, 2, 3).reshape(ATTN_WIDTH, D).astype(BF16))
    sinks = attn_sinks[l].astype(F32).reshape(N_KV_HEADS, Q_PER_KV).T.reshape(N_Q_HEADS)
    ws = jnp.where(causal[None], gmlp_w_s[l], 0).astype(BF16)
    btab = jnp.repeat(gmlp_b_s[l].astype(F32).T, GMLP_WIDTH // GMLP_GROUPS, axis=1)
    x1, xn = _mixer(x, sinks, mix_norm_w[l].reshape(1, D), win, bias, wba,
                    gmlp_ln_w[l].reshape(1, -1), gmlp_ln_b[l].reshape(1, -1), ws, btab,
                    w_branch_gmlp[l].astype(BF16), w_out[l].astype(BF16), ffn_norm_w[l].reshape(1, D))

    wqt = peer_w_q[l].reshape(D, PEER_HEADS, 2, PEER_HALF).transpose(2, 1, 3, 0).reshape(2 * nrow, D)
    k1kh = jnp.einsum('kd,hg->khgd', peer_keys_1[l], eye).reshape(nrow, nrow).astype(BF16)
    k2kh = jnp.einsum('kd,hg->khgd', peer_keys_2[l], eye).reshape(nrow, nrow).astype(BF16)
    k2hk = jnp.einsum('kd,hg->hkgd', peer_keys_2[l], eye).reshape(nrow, nrow).astype(BF16)
    xn2 = xn.reshape(B * S, D)
    thr, cc, s2, e2 = _peer_prep(xn2, wqt.astype(BF16), k1kh, k2kh, k2hk)
    vt = peer_v[l].reshape(PEER_N_EXPERTS // PEER_EC, PEER_EC, D).transpose(0, 2, 1).astype(BF16)
    y = _peer(xn2, thr, cc, s2, e2, peer_u[l], vt,
              x1.reshape(B * S, D), final_norm_w.reshape(1, D))
    return y.reshape(B, S, D)
```

```python
import functools

import numpy as np
import jax
import jax.numpy as jnp
from jax import lax
from jax.experimental import pallas as pl
from jax.experimental.pallas import tpu as pltpu

F32 = jnp.float32
BF16 = jnp.bfloat16

D_MODEL = 1024
HEAD_DIM = 64
N_Q_HEADS = 8
N_KV_HEADS = 2
Q_PER_KV = N_Q_HEADS // N_KV_HEADS
ATTN_WIDTH = N_Q_HEADS * HEAD_DIM
KV_WIDTH = N_KV_HEADS * HEAD_DIM
WINDOW = 128
BLK = 128
N_BUCKETS = 32
MAX_DISTANCE = 128
GMLP_GROUPS = 8
GMLP_WIDTH = 512
PEER_HEADS = 8
PEER_NKEYS = 128
PEER_N_EXPERTS = PEER_NKEYS * PEER_NKEYS
PEER_HALF = 128
PEER_TOPK = 16
EPS = 1e-6
NEG_INF = -1e30
_GELU_C0 = float(np.sqrt(2.0 / np.pi))
_GELU_C1 = 0.044715 * _GELU_C0

_C_Q, _C_K, _C_V, _C_GU, _C_GV, _C_GA, _C_GB, _C_END = 0, 512, 640, 768, 1280, 1792, 2816, 3840

MIX_TM = 512
PREP_TT = 512
PEER_TT = 512
PEER_EC = 1024
GATE_ROWS = 32
VMEM_LIMIT = 56 * 1024 * 1024


def _rms(x, w):
    return x * lax.rsqrt(jnp.mean(x * x, axis=-1, keepdims=True) + EPS) * w


def _mixer_kernel(sink_ref, x_ref, nw_ref, win_ref, bias_ref, wba_ref, lnw_ref, lnb_ref, ws_ref,
                  btab_ref, wbg_ref, wout_ref, fnw_ref, x1_ref, xn_ref,
                  q_scr, kext, vext, gu_scr, gv_scr, ya_scr, yb_scr):
    j = pl.program_id(1)
    nb = MIX_TM // BLK
    x = x_ref[0]
    h = _rms(x, nw_ref[...]).astype(BF16)

    def proj(lo, hi):
        return jnp.dot(h, win_ref[:, lo:hi], preferred_element_type=F32)

    q_scr[...] = proj(_C_Q, _C_K) * (HEAD_DIM ** -0.5)

    @pl.when(j == 0)
    def _():
        kext[0:BLK, :] = jnp.zeros((BLK, KV_WIDTH), BF16)
        vext[0:BLK, :] = jnp.zeros((BLK, KV_WIDTH), BF16)

    kext[BLK:, :] = proj(_C_K, _C_V).astype(BF16)
    vext[BLK:, :] = proj(_C_V, _C_GU).astype(BF16)
    gu_scr[...] = jax.nn.gelu(proj(_C_GU, _C_GV))
    gv = jax.nn.gelu(proj(_C_GV, _C_GA))
    mu = jnp.mean(gv, axis=-1, keepdims=True)
    gc = gv - mu
    gvn = gc * lax.rsqrt(jnp.mean(gc * gc, axis=-1, keepdims=True) + EPS)
    gv_scr[...] = (gvn * lnw_ref[...] + lnb_ref[...]).astype(BF16)

    lane = lax.broadcasted_iota(jnp.int32, (BLK, BLK), 1)
    low = lane < HEAD_DIM
    col = lax.broadcasted_iota(jnp.int32, (BLK, 2 * BLK), 1)

    def block(b, carry):
        r0 = pl.multiple_of(b * BLK, BLK)
        qb = q_scr[pl.ds(r0, BLK), :]
        kb = kext[pl.ds(r0, 2 * BLK), :]
        vb = vext[pl.ds(r0, 2 * BLK), :]
        parts = []
        for jj in range(4):
            p_j = qb[:, jj * BLK:(jj + 1) * BLK]
            parts.append(jnp.where(low, p_j, 0.0).astype(BF16))
            parts.append(jnp.where(low, 0.0, p_j).astype(BF16))
        qs = jnp.concatenate(parts, axis=0)
        logits = lax.dot_general(qs, kb, (((1,), (1,)), ((), ())), preferred_element_type=F32)
        left_cut = jnp.where(j * nb + b == 0, BLK, 0)
        outs = []
        for r in range(8):
            lg = logits[r * BLK:(r + 1) * BLK, :] + bias_ref[r * BLK:(r + 1) * BLK, :]
            lg = jnp.where(col < left_cut, NEG_INF, lg)
            s = sink_ref[r]
            m = jnp.maximum(jnp.max(lg, axis=-1, keepdims=True), s)
            p = jnp.exp(lg - m)
            den = jnp.sum(p, axis=-1, keepdims=True) + jnp.exp(s - m)
            pv = jnp.dot(p.astype(BF16), vb, preferred_element_type=F32)
            outs.append(pv / den)
        pairs = [jnp.where(low, outs[2 * jj], outs[2 * jj + 1]) for jj in range(4)]
        ya_scr[pl.ds(r0, BLK), :] = jnp.concatenate(pairs, axis=1).astype(BF16)

        gub = gu_scr[pl.ds(r0, BLK), :]
        vnb = gv_scr[pl.ds(r0, BLK), :]
        ys = []
        for mm in range(4):
            vblk = vnb[:, mm * BLK:(mm + 1) * BLK]
            s0 = jnp.dot(ws_ref[2 * mm], vblk, preferred_element_type=F32)
            s1 = jnp.dot(ws_ref[2 * mm + 1], vblk, preferred_element_type=F32)
            sg = jnp.where(low, s0, s1) + btab_ref[:, mm * BLK:(mm + 1) * BLK]
            ys.append(gub[:, mm * BLK:(mm + 1) * BLK] * sg)
        yb_scr[pl.ds(r0, BLK), :] = jnp.concatenate(ys, axis=1).astype(BF16)
        return carry

    lax.fori_loop(0, nb, block, 0)

    kext[0:BLK, :] = kext[MIX_TM:MIX_TM + BLK, :]
    vext[0:BLK, :] = vext[MIX_TM:MIX_TM + BLK, :]

    ya_p = jnp.dot(ya_scr[...], wba_ref[...], preferred_element_type=F32)
    yb_p = jnp.dot(yb_scr[...], wbg_ref[...], preferred_element_type=F32)
    merged = jax.nn.sigmoid(proj(_C_GA, _C_GB)) * ya_p + jax.nn.sigmoid(proj(_C_GB, _C_END)) * yb_p
    x1 = x + jnp.dot(merged.astype(BF16), wout_ref[...], preferred_element_type=F32)
    x1_ref[0] = x1
    xn_ref[0] = _rms(x1, fnw_ref[...]).astype(BF16)


def _const_spec(shape):
    nd = len(shape)
    return pl.BlockSpec(shape, lambda *_: (0,) * nd)


def _mixer(x, sinks, nw, win, bias, wba, lnw, lnb, ws, btab, wbg, wout, fnw):
    B, S, D = x.shape
    grid = (B, S // MIX_TM)
    tile = pl.BlockSpec((1, MIX_TM, D), lambda b, j: (b, j, 0))
    in_specs = [
        pl.BlockSpec(memory_space=pltpu.SMEM),
        tile,
        _const_spec(nw.shape), _const_spec(win.shape), _const_spec(bias.shape), _const_spec(wba.shape),
        _const_spec(lnw.shape), _const_spec(lnb.shape), _const_spec(ws.shape), _const_spec(btab.shape),
        _const_spec(wbg.shape), _const_spec(wout.shape), _const_spec(fnw.shape),
    ]
    return pl.pallas_call(
        _mixer_kernel,
        grid=grid,
        in_specs=in_specs,
        out_specs=[tile, tile],
        out_shape=[jax.ShapeDtypeStruct((B, S, D), F32), jax.ShapeDtypeStruct((B, S, D), BF16)],
        scratch_shapes=[
            pltpu.VMEM((MIX_TM, ATTN_WIDTH), F32),
            pltpu.VMEM((MIX_TM + BLK, KV_WIDTH), BF16),
            pltpu.VMEM((MIX_TM + BLK, KV_WIDTH), BF16),
            pltpu.VMEM((MIX_TM, GMLP_WIDTH), F32),
            pltpu.VMEM((MIX_TM, GMLP_WIDTH), BF16),
            pltpu.VMEM((MIX_TM, ATTN_WIDTH), BF16),
            pltpu.VMEM((MIX_TM, GMLP_WIDTH), BF16),
        ],
        compiler_params=pltpu.CompilerParams(
            dimension_semantics=("arbitrary", "arbitrary"), vmem_limit_bytes=VMEM_LIMIT),
        name="mixer",
    )(sinks, x, nw, win, bias, wba, lnw, lnb, ws, btab, wbg, wout, fnw)


def _cmpx(a, b):
    if a is None:
        return b, None
    if b is None:
        return a, None
    return jnp.maximum(a, b), jnp.minimum(a, b)


def _bitonic_sort_desc(v):
    v = list(v)
    n = len(v)
    k = 2
    while k <= n:
        jj = k // 2
        while jj >= 1:
            for i in range(n):
                l = i ^ jj
                if l > i:
                    hi, lo = _cmpx(v[i], v[l])
                    if (i & k) == 0:
                        v[i], v[l] = hi, lo
                    else:
                        v[i], v[l] = lo, hi
            jj //= 2
        k *= 2
    return v


def _merge_top(a, b):
    n = len(a)
    c = [_cmpx(a[i], b[n - 1 - i])[0] for i in range(n)]
    jj = n // 2
    while jj >= 1:
        for i in range(n):
            if (i & jj) == 0:
                c[i], c[i + jj] = _cmpx(c[i], c[i + jj])
        jj //= 2
    return c


def _top_sorted(vals, k):
    vals = list(vals)
    while len(vals) % k:
        vals.append(None)
    groups = [_bitonic_sort_desc(vals[g:g + k]) for g in range(0, len(vals), k)]
    while len(groups) > 1:
        nxt = [_merge_top(groups[g], groups[g + 1]) for g in range(0, len(groups) - 1, 2)]
        if len(groups) % 2:
            nxt.append(groups[-1])
        groups = nxt
    return groups[0]


def _max_below(vals, bound):
    m = None
    for v in vals:
        w = jnp.where(v < bound, v, -3.0e38)
        m = w if m is None else jnp.maximum(m, w)
    return m


def _prep_kernel(xn_ref, wqt_ref, k1_ref, k2kh_ref, k2_ref, thr_ref, c_ref, s2_ref, e2_ref,
                 s1_scr, s2kh_scr, s2hk_scr, b0_scr):
    nlb = PREP_TT // BLK
    xn = xn_ref[...]
    qt = lax.dot_general(wqt_ref[...], xn, (((1,), (1,)), ((), ())), preferred_element_type=F32)
    nrow = PEER_HEADS * PEER_HALF
    q1 = qt[:nrow].astype(BF16)
    q2 = qt[nrow:].astype(BF16)
    s1 = jnp.dot(k1_ref[...], q1, preferred_element_type=F32)
    s2kh = jnp.dot(k2kh_ref[...], q2, preferred_element_type=F32)
    for lb in range(nlb):
        s1_scr[lb] = s1[:, lb * BLK:(lb + 1) * BLK]
        s2kh_scr[lb] = s2kh[:, lb * BLK:(lb + 1) * BLK]
    for hh in range(PEER_HEADS):
        rows = slice(hh * PEER_NKEYS, (hh + 1) * PEER_NKEYS)
        s2h = jnp.dot(k2_ref[...], q2[hh * PEER_HALF:(hh + 1) * PEER_HALF, :],
                      preferred_element_type=F32)
        for lb in range(nlb):
            s2hk_scr[lb, rows, :] = s2h[:, lb * BLK:(lb + 1) * BLK]

    def lane_block(lb, carry):
        s1_rows = [s1_scr[lb, k * 8:(k + 1) * 8, :] for k in range(PEER_NKEYS)]
        s2_rows = [s2kh_scr[lb, k * 8:(k + 1) * 8, :] for k in range(PEER_NKEYS)]
        a = _top_sorted(s1_rows, PEER_TOPK)
        b = _top_sorted(s2_rows, PEER_TOPK)
        a_next = _max_below(s1_rows, a[PEER_TOPK - 1])
        b_next = _max_below(s2_rows, b[PEER_TOPK - 1])
        cands = [a[i] + b[jx] for i in range(PEER_TOPK) for jx in range(PEER_TOPK)
                 if (i + 1) * (jx + 1) <= PEER_TOPK]
        top = _top_sorted(cands, PEER_TOPK)
        runner_up = _max_below(cands + [a_next + b[0], a[0] + b_next], top[PEER_TOPK - 1])
        tau = 0.5 * (top[PEER_TOPK - 1] + runner_up)
        z = jnp.ones_like(tau)
        for kk in range(1, PEER_TOPK):
            z = z + jnp.exp(top[kk] - top[0])
        inv_z = 0.5 / z
        tau_rel = tau - b[0]
        for k in range(PEER_NKEYS):
            s1k = s1_scr[lb, k * 8:(k + 1) * 8, :]
            thr_ref[lb, k * 8:(k + 1) * 8, :] = tau_rel - s1k
            c_ref[lb, k * 8:(k + 1) * 8, :] = jnp.exp(s1k - a[0]) * inv_z
        b0_scr[lb] = b[0]
        return carry

    lax.fori_loop(0, nlb, lane_block, 0)

    for lb in range(nlb):
        for hh in range(PEER_HEADS):
            rows = slice(hh * PEER_NKEYS, (hh + 1) * PEER_NKEYS)
            s2_rel = s2hk_scr[lb, rows, :] - b0_scr[lb, hh:hh + 1, :]
            s2_ref[lb, rows, :] = s2_rel
            e2_ref[lb, rows, :] = jnp.exp(s2_rel)


def _peer_prep(xn, wqt, k1kh, k2kh, k2):
    T, D = xn.shape
    nrow = PEER_HEADS * PEER_NKEYS
    nlb = PREP_TT // BLK
    out_spec = pl.BlockSpec((nlb, nrow, BLK), lambda i: (i, 0, 0))
    rows_f32 = jax.ShapeDtypeStruct((T // BLK, nrow, BLK), F32)
    return pl.pallas_call(
        _prep_kernel,
        grid=(T // PREP_TT,),
        in_specs=[pl.BlockSpec((PREP_TT, D), lambda i: (i, 0)),
                  _const_spec(wqt.shape), _const_spec(k1kh.shape), _const_spec(k2kh.shape),
                  _const_spec(k2.shape)],
        out_specs=[out_spec] * 4,
        out_shape=[rows_f32] * 4,
        scratch_shapes=[pltpu.VMEM((nlb, nrow, BLK), F32), pltpu.VMEM((nlb, nrow, BLK), F32),
                        pltpu.VMEM((nlb, nrow, BLK), F32), pltpu.VMEM((nlb, PEER_HEADS, BLK), F32)],
        compiler_params=pltpu.CompilerParams(
            dimension_semantics=("arbitrary",), vmem_limit_bytes=VMEM_LIMIT),
        name="peer_prep",
    )(xn, wqt, k1kh, k2kh, k2)


def _peer_stage(c_gate, xn_ref, thr_ref, c_ref, s2_ref, e2_ref, u_ref, vt_ref, yt_scr,
                ht_w, ht_r, at_w, at_r, *, first_matmul=True, gate=True, second_matmul=True):
    n_i1 = PEER_EC // PEER_NKEYS
    nlb = PEER_TT // BLK
    if second_matmul:
        yt_scr[...] += jnp.dot(vt_ref[0], at_r[...], preferred_element_type=F32)
    if first_matmul:
        ht_w[...] = lax.dot_general(u_ref[...], xn_ref[...], (((1,), (1,)), ((), ())),
                                    preferred_element_type=F32)
    if not gate:
        return
    row0 = c_gate * (n_i1 * PEER_HEADS)
    for lb in range(nlb):
        lanes = slice(lb * BLK, (lb + 1) * BLK)
        for rb in range(PEER_NKEYS // GATE_ROWS):
            gates = [jnp.zeros((GATE_ROWS, BLK), F32) for _ in range(n_i1)]
            for hh in range(PEER_HEADS):
                rows = slice(hh * PEER_NKEYS + rb * GATE_ROWS, hh * PEER_NKEYS + (rb + 1) * GATE_ROWS)
                s2 = s2_ref[lb, rows, :]
                e2 = e2_ref[lb, rows, :]
                for ii in range(n_i1):
                    r = row0 + ii * PEER_HEADS + hh
                    thr = thr_ref[lb, pl.ds(r, 1), :]
                    cc = c_ref[lb, pl.ds(r, 1), :]
                    gates[ii] = gates[ii] + jnp.where(s2 >= thr, e2, 0.0) * cc
            for ii in range(n_i1):
                rows = slice(ii * PEER_NKEYS + rb * GATE_ROWS, ii * PEER_NKEYS + (rb + 1) * GATE_ROWS)
                x = ht_r[rows, lanes]
                t = jnp.tanh(x * (_GELU_C0 + _GELU_C1 * (x * x)))
                at_w[rows, lanes] = ((x + x * t) * gates[ii]).astype(BF16)


def _peer_kernel(xn_ref, thr_ref, c_ref, s2_ref, e2_ref, u_ref, vt_ref, x1_ref, fw_ref, out_ref,
                 yt_scr, ht_a, ht_b, at_a, at_b):
    s = pl.program_id(1)
    n_chunks = PEER_N_EXPERTS // PEER_EC

    @pl.when(s == 0)
    def _():
        yt_scr[...] = jnp.zeros_like(yt_scr)

    c_gate = jnp.clip(s - 1, 0, n_chunks - 1)
    stage = functools.partial(_peer_stage, c_gate, xn_ref, thr_ref, c_ref, s2_ref, e2_ref, u_ref,
                              vt_ref, yt_scr)
    even = lax.rem(s, 2) == 0
    inner = jnp.logical_and(s >= 2, s < n_chunks)

    @pl.when(jnp.logical_and(even, inner))
    def _():
        stage(ht_a, ht_b, at_b, at_a)

    @pl.when(jnp.logical_and(jnp.logical_not(even), inner))
    def _():
        stage(ht_b, ht_a, at_a, at_b)

    @pl.when(s == 0)
    def _():
        stage(ht_a, ht_b, at_b, at_a, gate=False, second_matmul=False)

    @pl.when(s == 1)
    def _():
        stage(ht_b, ht_a, at_a, at_b, second_matmul=False)

    @pl.when(s == n_chunks)
    def _():
        stage(ht_a, ht_b, at_b, at_a, first_matmul=False)

    @pl.when(s == n_chunks + 1)
    def _():
        stage(ht_b, ht_a, at_a, at_b, first_matmul=False, gate=False)

    @pl.when(s == pl.num_programs(1) - 1)
    def _():
        xo = x1_ref[...] + yt_scr[...].T
        out_ref[...] = _rms(xo, fw_ref[...])


def _peer(xn, thr, cc, s2, e2, u, vt, x1, fw):
    T, D = xn.shape
    nrow = PEER_HEADS * PEER_NKEYS
    nlb = PEER_TT // BLK
    n_chunks = PEER_N_EXPERTS // PEER_EC
    assert n_chunks % 2 == 0 and n_chunks >= 4
    prep_spec = pl.BlockSpec((nlb, nrow, BLK), lambda i, s: (i, 0, 0))
    tok_spec = pl.BlockSpec((PEER_TT, D), lambda i, s: (i, 0))
    return pl.pallas_call(
        _peer_kernel,
        grid=(T // PEER_TT, n_chunks + 2),
        in_specs=[tok_spec, prep_spec, prep_spec, prep_spec, prep_spec,
                  pl.BlockSpec((PEER_EC, D), lambda i, s: (jnp.minimum(s, n_chunks - 1), 0)),
                  pl.BlockSpec((1, D, PEER_EC), lambda i, s: (jnp.clip(s - 2, 0, n_chunks - 1), 0, 0)),
                  tok_spec, _const_spec(fw.shape)],
        out_specs=tok_spec,
        out_shape=jax.ShapeDtypeStruct((T, D), F32),
        scratch_shapes=[pltpu.VMEM((D, PEER_TT), F32),
                        pltpu.VMEM((PEER_EC, PEER_TT), F32), pltpu.VMEM((PEER_EC, PEER_TT), F32),
                        pltpu.VMEM((PEER_EC, PEER_TT), BF16), pltpu.VMEM((PEER_EC, PEER_TT), BF16)],
        compiler_params=pltpu.CompilerParams(
            dimension_semantics=("arbitrary", "arbitrary"), vmem_limit_bytes=VMEM_LIMIT),
        name="peer_ffn",
    )(xn, thr, cc, s2, e2, u, vt, x1, fw)


def _t5_bucket(dist):
    n = np.maximum(dist, 0)
    max_exact = N_BUCKETS // 2
    large = max_exact + (np.log(np.maximum(n, 1) / max_exact) / np.log(MAX_DISTANCE / max_exact)
                         * (N_BUCKETS - max_exact)).astype(np.int32)
    large = np.minimum(large, N_BUCKETS - 1)
    return np.where(n < max_exact, n, large).astype(np.int32)


_PAIR_HEADS = [h for jj in range(Q_PER_KV) for h in (jj, Q_PER_KV + jj)]


def _band_bias(rel_bias):
    qi = np.arange(BLK)[:, None]
    kj = np.arange(2 * BLK)[None, :]
    dist = qi + BLK - kj
    in_window = (dist >= 0) & (dist < WINDOW)
    onehot = (_t5_bucket(dist)[:, :, None] == np.arange(N_BUCKETS)).astype(np.float32)
    per_head = rel_bias.astype(F32)[:, np.array(_PAIR_HEADS)]
    bias = jnp.einsum('qkb,bh->hqk', onehot, per_head, precision=lax.Precision.HIGHEST)
    bias = jnp.where(in_window[None], bias, NEG_INF)
    return bias.reshape(N_Q_HEADS * BLK, 2 * BLK)


def kernel(x, mix_norm_w, w_in, rel_bias, attn_sinks, w_branch_attn, gmlp_ln_w, gmlp_ln_b, gmlp_w_s,
           gmlp_b_s, w_branch_gmlp, w_out, ffn_norm_w, peer_w_q, peer_keys_1, peer_keys_2, peer_u,
           peer_v, final_norm_w):
    B, S, D = x.shape
    assert w_in.shape[0] == 1, "single-layer block"
    l = 0
    bias = _band_bias(rel_bias)
    causal = np.tril(np.ones((BLK, BLK), dtype=bool))
    eye = jnp.eye(PEER_HEADS, dtype=F32)
    nrow = PEER_HEADS * PEER_NKEYS

    w_q_pairs = (w_in[l][:, :ATTN_WIDTH].reshape(D, N_KV_HEADS, Q_PER_KV, HEAD_DIM)
                 .transpose(0, 2, 1, 3).reshape(D, ATTN_WIDTH))
    win = jnp.concatenate([w_q_pairs, w_in[l][:, ATTN_WIDTH:]], axis=1).astype(BF16)
    wba = (w_branch_attn[l].reshape(N_KV_HEADS, Q_PER_KV, HEAD_DIM, D)
           .transpose(1, 0, 2, 3).reshape(ATTN_WIDTH, D).astype(BF16))
    sinks = attn_sinks[l].astype(F32).reshape(N_KV_HEADS, Q_PER_KV).T.reshape(N_Q_HEADS)
    ws = jnp.where(causal[None], gmlp_w_s[l], 0).astype(BF16)
    btab = jnp.repeat(gmlp_b_s[l].astype(F32).T, GMLP_WIDTH // GMLP_GROUPS, axis=1)
    x1, xn = _mixer(x, sinks, mix_norm_w[l].reshape(1, D), win, bias, wba,
                    gmlp_ln_w[l].reshape(1, -1), gmlp_ln_b[l].reshape(1, -1), ws, btab,
                    w_branch_gmlp[l].astype(BF16), w_out[l].astype(BF16), ffn_norm_w[l].reshape(1, D))

    wqt = peer_w_q[l].reshape(D, PEER_HEADS, 2, PEER_HALF).transpose(2, 1, 3, 0).reshape(2 * nrow, D)
    k1kh = jnp.einsum('kd,hg->khgd', peer_keys_1[l], eye).reshape(nrow, nrow).astype(BF16)
    k2kh = jnp.einsum('kd,hg->khgd', peer_keys_2[l], eye).reshape(nrow, nrow).astype(BF16)
    xn2 = xn.reshape(B * S, D)
    thr, cc, s2, e2 = _peer_prep(xn2, wqt.astype(BF16), k1kh, k2kh, peer_keys_2[l].astype(BF16))
    vt = peer_v[l].reshape(PEER_N_EXPERTS // PEER_EC, PEER_EC, D).transpose(0, 2, 1).astype(BF16)
    y = _peer(xn2, thr, cc, s2, e2, peer_u[l], vt,
              x1.reshape(B * S, D), final_norm_w.reshape(1, D))
    return y.reshape(B, S, D)
```

```python
import functools

import numpy as np
import jax
import jax.numpy as jnp
from jax import lax
from jax.experimental import pallas as pl
from jax.experimental.pallas import tpu as pltpu

F32 = jnp.float32
BF16 = jnp.bfloat16

D_MODEL = 1024
HEAD_DIM = 64
N_Q_HEADS = 8
N_KV_HEADS = 2
Q_PER_KV = N_Q_HEADS // N_KV_HEADS
ATTN_WIDTH = N_Q_HEADS * HEAD_DIM
KV_WIDTH = N_KV_HEADS * HEAD_DIM
WINDOW = 128
BLK = 128
N_BUCKETS = 32
MAX_DISTANCE = 128
GMLP_GROUPS = 8
GMLP_WIDTH = 512
PEER_HEADS = 8
PEER_NKEYS = 128
PEER_N_EXPERTS = PEER_NKEYS * PEER_NKEYS
PEER_HALF = 128
PEER_TOPK = 16
EPS = 1e-6
NEG_INF = -1e30
_GELU_C0 = float(np.sqrt(2.0 / np.pi))
_GELU_C1 = 0.044715 * _GELU_C0

_C_Q, _C_K, _C_V, _C_GU, _C_GV, _C_GA, _C_GB, _C_END = 0, 512, 640, 768, 1280, 1792, 2816, 3840

MIX_TM = 512
PREP_TT = 512
PEER_TT = 512
PEER_EC = 1024
GATE_ROWS = 32
VMEM_LIMIT = 56 * 1024 * 1024


def _rms(x, w):
    return x * lax.rsqrt(jnp.mean(x * x, axis=-1, keepdims=True) + EPS) * w


def _mixer_kernel(sink_ref, x_ref, nw_ref, win_ref, bias_ref, wba_ref, lnw_ref, lnb_ref, ws_ref,
                  btab_ref, wbg_ref, wout_ref, fnw_ref, x1_ref, xn_ref,
                  q_scr, kext, vext, gu_scr, gv_scr, ya_scr, yb_scr):
    j = pl.program_id(1)
    nb = MIX_TM // BLK
    x = x_ref[0]
    h = _rms(x, nw_ref[...]).astype(BF16)

    def proj(lo, hi):
        return jnp.dot(h, win_ref[:, lo:hi], preferred_element_type=F32)

    q_scr[...] = proj(_C_Q, _C_K) * (HEAD_DIM ** -0.5)

    @pl.when(j == 0)
    def _():
        kext[0:BLK, :] = jnp.zeros((BLK, KV_WIDTH), BF16)
        vext[0:BLK, :] = jnp.zeros((BLK, KV_WIDTH), BF16)

    kext[BLK:, :] = proj(_C_K, _C_V).astype(BF16)
    vext[BLK:, :] = proj(_C_V, _C_GU).astype(BF16)
    gu_scr[...] = jax.nn.gelu(proj(_C_GU, _C_GV))
    gv = jax.nn.gelu(proj(_C_GV, _C_GA))
    mu = jnp.mean(gv, axis=-1, keepdims=True)
    gc = gv - mu
    gvn = gc * lax.rsqrt(jnp.mean(gc * gc, axis=-1, keepdims=True) + EPS)
    gv_scr[...] = (gvn * lnw_ref[...] + lnb_ref[...]).astype(BF16)

    lane = lax.broadcasted_iota(jnp.int32, (BLK, BLK), 1)
    low = lane < HEAD_DIM
    col = lax.broadcasted_iota(jnp.int32, (BLK, 2 * BLK), 1)

    def block(b, carry):
        r0 = b * BLK
        qb = q_scr[pl.ds(r0, BLK), :]
        kb = kext[pl.ds(r0, 2 * BLK), :]
        vb = vext[pl.ds(r0, 2 * BLK), :]
        parts = []
        for jj in range(4):
            p_j = qb[:, jj * BLK:(jj + 1) * BLK]
            parts.append(jnp.where(low, p_j, 0.0).astype(BF16))
            parts.append(jnp.where(low, 0.0, p_j).astype(BF16))
        qs = jnp.concatenate(parts, axis=0)
        logits = lax.dot_general(qs, kb, (((1,), (1,)), ((), ())), preferred_element_type=F32)
        left_cut = jnp.where(j * nb + b == 0, BLK, 0)
        outs = []
        for r in range(8):
            lg = logits[r * BLK:(r + 1) * BLK, :] + bias_ref[r * BLK:(r + 1) * BLK, :]
            lg = jnp.where(col < left_cut, NEG_INF, lg)
            s = sink_ref[r]
            m = jnp.maximum(jnp.max(lg, axis=-1, keepdims=True), s)
            p = jnp.exp(lg - m)
            den = jnp.sum(p, axis=-1, keepdims=True) + jnp.exp(s - m)
            pv = jnp.dot(p.astype(BF16), vb, preferred_element_type=F32)
            outs.append(pv / den)
        pairs = [jnp.where(low, outs[2 * jj], outs[2 * jj + 1]) for jj in range(4)]
        ya_scr[pl.ds(r0, BLK), :] = jnp.concatenate(pairs, axis=1).astype(BF16)

        gub = gu_scr[pl.ds(r0, BLK), :]
        vnb = gv_scr[pl.ds(r0, BLK), :]
        ys = []
        for mm in range(4):
            vblk = vnb[:, mm * BLK:(mm + 1) * BLK]
            s0 = jnp.dot(ws_ref[2 * mm], vblk, preferred_element_type=F32)
            s1 = jnp.dot(ws_ref[2 * mm + 1], vblk, preferred_element_type=F32)
            sg = jnp.where(low, s0, s1) + btab_ref[:, mm * BLK:(mm + 1) * BLK]
            ys.append(gub[:, mm * BLK:(mm + 1) * BLK] * sg)
        yb_scr[pl.ds(r0, BLK), :] = jnp.concatenate(ys, axis=1).astype(BF16)
        return carry

    for b_static in range(nb):
        block(b_static, 0)

    kext[0:BLK, :] = kext[MIX_TM:MIX_TM + BLK, :]
    vext[0:BLK, :] = vext[MIX_TM:MIX_TM + BLK, :]

    ya_p = jnp.dot(ya_scr[...], wba_ref[...], preferred_element_type=F32)
    yb_p = jnp.dot(yb_scr[...], wbg_ref[...], preferred_element_type=F32)
    merged = jax.nn.sigmoid(proj(_C_GA, _C_GB)) * ya_p + jax.nn.sigmoid(proj(_C_GB, _C_END)) * yb_p
    x1 = x + jnp.dot(merged.astype(BF16), wout_ref[...], preferred_element_type=F32)
    x1_ref[0] = x1
    xn_ref[0] = _rms(x1, fnw_ref[...]).astype(BF16)


def _const_spec(shape):
    nd = len(shape)
    return pl.BlockSpec(shape, lambda *_: (0,) * nd)


def _mixer(x, sinks, nw, win, bias, wba, lnw, lnb, ws, btab, wbg, wout, fnw):
    B, S, D = x.shape
    grid = (B, S // MIX_TM)
    tile = pl.BlockSpec((1, MIX_TM, D), lambda b, j: (b, j, 0))
    in_specs = [
        pl.BlockSpec(memory_space=pltpu.SMEM),
        tile,
        _const_spec(nw.shape), _const_spec(win.shape), _const_spec(bias.shape), _const_spec(wba.shape),
        _const_spec(lnw.shape), _const_spec(lnb.shape), _const_spec(ws.shape), _const_spec(btab.shape),
        _const_spec(wbg.shape), _const_spec(wout.shape), _const_spec(fnw.shape),
    ]
    return pl.pallas_call(
        _mixer_kernel,
        grid=grid,
        in_specs=in_specs,
        out_specs=[tile, tile],
        out_shape=[jax.ShapeDtypeStruct((B, S, D), F32), jax.ShapeDtypeStruct((B, S, D), BF16)],
        scratch_shapes=[
            pltpu.VMEM((MIX_TM, ATTN_WIDTH), F32),
            pltpu.VMEM((MIX_TM + BLK, KV_WIDTH), BF16),
            pltpu.VMEM((MIX_TM + BLK, KV_WIDTH), BF16),
            pltpu.VMEM((MIX_TM, GMLP_WIDTH), F32),
            pltpu.VMEM((MIX_TM, GMLP_WIDTH), BF16),
            pltpu.VMEM((MIX_TM, ATTN_WIDTH), BF16),
            pltpu.VMEM((MIX_TM, GMLP_WIDTH), BF16),
        ],
        compiler_params=pltpu.CompilerParams(
            dimension_semantics=("arbitrary", "arbitrary"), vmem_limit_bytes=VMEM_LIMIT),
        name="mixer",
    )(sinks, x, nw, win, bias, wba, lnw, lnb, ws, btab, wbg, wout, fnw)


def _cmpx(a, b):
    if a is None:
        return b, None
    if b is None:
        return a, None
    return jnp.maximum(a, b), jnp.minimum(a, b)


def _bitonic_sort_desc(v):
    v = list(v)
    n = len(v)
    k = 2
    while k <= n:
        jj = k // 2
        while jj >= 1:
            for i in range(n):
                l = i ^ jj
                if l > i:
                    hi, lo = _cmpx(v[i], v[l])
                    if (i & k) == 0:
                        v[i], v[l] = hi, lo
                    else:
                        v[i], v[l] = lo, hi
            jj //= 2
        k *= 2
    return v


def _merge_top(a, b):
    n = len(a)
    c = [_cmpx(a[i], b[n - 1 - i])[0] for i in range(n)]
    jj = n // 2
    while jj >= 1:
        for i in range(n):
            if (i & jj) == 0:
                c[i], c[i + jj] = _cmpx(c[i], c[i + jj])
        jj //= 2
    return c


def _top_sorted(vals, k):
    vals = list(vals)
    while len(vals) % k:
        vals.append(None)
    groups = [_bitonic_sort_desc(vals[g:g + k]) for g in range(0, len(vals), k)]
    while len(groups) > 1:
        nxt = [_merge_top(groups[g], groups[g + 1]) for g in range(0, len(groups) - 1, 2)]
        if len(groups) % 2:
            nxt.append(groups[-1])
        groups = nxt
    return groups[0]


def _max_below(vals, bound):
    m = None
    for v in vals:
        w = jnp.where(v < bound, v, -3.0e38)
        m = w if m is None else jnp.maximum(m, w)
    return m


def _prep_kernel(xn_ref, wqt_ref, k1_ref, k2kh_ref, k2_ref, thr_ref, c_ref, s2_ref, e2_ref,
                 s1_scr, s2kh_scr, s2hk_scr, b0_scr):
    nlb = PREP_TT // BLK
    xn = xn_ref[...]
    qt = lax.dot_general(wqt_ref[...], xn, (((1,), (1,)), ((), ())), preferred_element_type=F32)
    nrow = PEER_HEADS * PEER_HALF
    q1 = qt[:nrow].astype(BF16)
    q2 = qt[nrow:].astype(BF16)
    s1 = jnp.dot(k1_ref[...], q1, preferred_element_type=F32)
    s2kh = jnp.dot(k2kh_ref[...], q2, preferred_element_type=F32)
    for lb in range(nlb):
        s1_scr[lb] = s1[:, lb * BLK:(lb + 1) * BLK]
        s2kh_scr[lb] = s2kh[:, lb * BLK:(lb + 1) * BLK]
    for hh in range(PEER_HEADS):
        rows = slice(hh * PEER_NKEYS, (hh + 1) * PEER_NKEYS)
        s2h = jnp.dot(k2_ref[...], q2[hh * PEER_HALF:(hh + 1) * PEER_HALF, :],
                      preferred_element_type=F32)
        for lb in range(nlb):
            s2hk_scr[lb, rows, :] = s2h[:, lb * BLK:(lb + 1) * BLK]

    def lane_block(lb, carry):
        s1_rows = [s1_scr[lb, k * 8:(k + 1) * 8, :] for k in range(PEER_NKEYS)]
        s2_rows = [s2kh_scr[lb, k * 8:(k + 1) * 8, :] for k in range(PEER_NKEYS)]
        a = _top_sorted(s1_rows, PEER_TOPK)
        b = _top_sorted(s2_rows, PEER_TOPK)
        a_next = _max_below(s1_rows, a[PEER_TOPK - 1])
        b_next = _max_below(s2_rows, b[PEER_TOPK - 1])
        cands = [a[i] + b[jx] for i in range(PEER_TOPK) for jx in range(PEER_TOPK)
                 if (i + 1) * (jx + 1) <= PEER_TOPK]
        top = _top_sorted(cands, PEER_TOPK)
        runner_up = _max_below(cands + [a_next + b[0], a[0] + b_next], top[PEER_TOPK - 1])
        tau = 0.5 * (top[PEER_TOPK - 1] + runner_up)
        z = jnp.ones_like(tau)
        for kk in range(1, PEER_TOPK):
            z = z + jnp.exp(top[kk] - top[0])
        inv_z = 0.5 / z
        tau_rel = tau - b[0]
        for k in range(PEER_NKEYS):
            s1k = s1_scr[lb, k * 8:(k + 1) * 8, :]
            thr_ref[lb, k * 8:(k + 1) * 8, :] = tau_rel - s1k
            c_ref[lb, k * 8:(k + 1) * 8, :] = jnp.exp(s1k - a[0]) * inv_z
        b0_scr[lb] = b[0]
        return carry

    lax.fori_loop(0, nlb, lane_block, 0)

    for lb in range(nlb):
        for hh in range(PEER_HEADS):
            rows = slice(hh * PEER_NKEYS, (hh + 1) * PEER_NKEYS)
            s2_rel = s2hk_scr[lb, rows, :] - b0_scr[lb, hh:hh + 1, :]
            s2_ref[lb, rows, :] = s2_rel
            e2_ref[lb, rows, :] = jnp.exp(s2_rel)


def _peer_prep(xn, wqt, k1kh, k2kh, k2):
    T, D = xn.shape
    nrow = PEER_HEADS * PEER_NKEYS
    nlb = PREP_TT // BLK
    out_spec = pl.BlockSpec((nlb, nrow, BLK), lambda i: (i, 0, 0))
    rows_f32 = jax.ShapeDtypeStruct((T // BLK, nrow, BLK), F32)
    return pl.pallas_call(
        _prep_kernel,
        grid=(T // PREP_TT,),
        in_specs=[pl.BlockSpec((PREP_TT, D), lambda i: (i, 0)),
                  _const_spec(wqt.shape), _const_spec(k1kh.shape), _const_spec(k2kh.shape),
                  _const_spec(k2.shape)],
        out_specs=[out_spec] * 4,
        out_shape=[rows_f32] * 4,
        scratch_shapes=[pltpu.VMEM((nlb, nrow, BLK), F32), pltpu.VMEM((nlb, nrow, BLK), F32),
                        pltpu.VMEM((nlb, nrow, BLK), F32), pltpu.VMEM((nlb, PEER_HEADS, BLK), F32)],
        compiler_params=pltpu.CompilerParams(
            dimension_semantics=("arbitrary",), vmem_limit_bytes=VMEM_LIMIT),
        name="peer_prep",
    )(xn, wqt, k1kh, k2kh, k2)


def _peer_stage(c_gate, xn_ref, thr_ref, c_ref, s2_ref, e2_ref, u_ref, vt_ref, yt_scr,
                ht_w, ht_r, at_w, at_r, *, first_matmul=True, gate=True, second_matmul=True):
    n_i1 = PEER_EC // PEER_NKEYS
    nlb = PEER_TT // BLK
    if second_matmul:
        yt_scr[...] += jnp.dot(vt_ref[0], at_r[...], preferred_element_type=F32)
    if first_matmul:
        ht_w[...] = lax.dot_general(u_ref[...], xn_ref[...], (((1,), (1,)), ((), ())),
                                    preferred_element_type=F32)
    if not gate:
        return
    row0 = c_gate * (n_i1 * PEER_HEADS)
    for lb in range(nlb):
        lanes = slice(lb * BLK, (lb + 1) * BLK)
        for rb in range(PEER_NKEYS // GATE_ROWS):
            gates = [jnp.zeros((GATE_ROWS, BLK), F32) for _ in range(n_i1)]
            for hh in range(PEER_HEADS):
                rows = slice(hh * PEER_NKEYS + rb * GATE_ROWS, hh * PEER_NKEYS + (rb + 1) * GATE_ROWS)
                s2 = s2_ref[lb, rows, :]
                e2 = e2_ref[lb, rows, :]
                for ii in range(n_i1):
                    r = row0 + ii * PEER_HEADS + hh
                    thr = thr_ref[lb, pl.ds(r, 1), :]
                    cc = c_ref[lb, pl.ds(r, 1), :]
                    gates[ii] = gates[ii] + jnp.where(s2 >= thr, e2, 0.0) * cc
            for ii in range(n_i1):
                rows = slice(ii * PEER_NKEYS + rb * GATE_ROWS, ii * PEER_NKEYS + (rb + 1) * GATE_ROWS)
                x = ht_r[rows, lanes]
                t = jnp.tanh(x * (_GELU_C0 + _GELU_C1 * (x * x)))
                at_w[rows, lanes] = ((x + x * t) * gates[ii]).astype(BF16)


def _peer_kernel(xn_ref, thr_ref, c_ref, s2_ref, e2_ref, u_ref, vt_ref, x1_ref, fw_ref, out_ref,
                 yt_scr, ht_a, ht_b, at_a, at_b):
    s = pl.program_id(1)
    n_chunks = PEER_N_EXPERTS // PEER_EC

    @pl.when(s == 0)
    def _():
        yt_scr[...] = jnp.zeros_like(yt_scr)

    c_gate = jnp.clip(s - 1, 0, n_chunks - 1)
    stage = functools.partial(_peer_stage, c_gate, xn_ref, thr_ref, c_ref, s2_ref, e2_ref, u_ref,
                              vt_ref, yt_scr)
    even = lax.rem(s, 2) == 0
    inner = jnp.logical_and(s >= 2, s < n_chunks)

    @pl.when(jnp.logical_and(even, inner))
    def _():
        stage(ht_a, ht_b, at_b, at_a)

    @pl.when(jnp.logical_and(jnp.logical_not(even), inner))
    def _():
        stage(ht_b, ht_a, at_a, at_b)

    @pl.when(s == 0)
    def _():
        stage(ht_a, ht_b, at_b, at_a, gate=False, second_matmul=False)

    @pl.when(s == 1)
    def _():
        stage(ht_b, ht_a, at_a, at_b, second_matmul=False)

    @pl.when(s == n_chunks)
    def _():
        stage(ht_a, ht_b, at_b, at_a, first_matmul=False)

    @pl.when(s == n_chunks + 1)
    def _():
        stage(ht_b, ht_a, at_a, at_b, first_matmul=False, gate=False)

    @pl.when(s == pl.num_programs(1) - 1)
    def _():
        xo = x1_ref[...] + yt_scr[...].T
        out_ref[...] = _rms(xo, fw_ref[...])


def _peer(xn, thr, cc, s2, e2, u, vt, x1, fw):
    T, D = xn.shape
    nrow = PEER_HEADS * PEER_NKEYS
    nlb = PEER_TT // BLK
    n_chunks = PEER_N_EXPERTS // PEER_EC
    assert n_chunks % 2 == 0 and n_chunks >= 4
    prep_spec = pl.BlockSpec((nlb, nrow, BLK), lambda i, s: (i, 0, 0))
    tok_spec = pl.BlockSpec((PEER_TT, D), lambda i, s: (i, 0))
    return pl.pallas_call(
        _peer_kernel,
        grid=(T // PEER_TT, n_chunks + 2),
        in_specs=[tok_spec, prep_spec, prep_spec, prep_spec, prep_spec,
                  pl.BlockSpec((PEER_EC, D), lambda i, s: (jnp.minimum(s, n_chunks - 1), 0)),
                  pl.BlockSpec((1, D, PEER_EC), lambda i, s: (jnp.clip(s - 2, 0, n_chunks - 1), 0, 0)),
                  tok_spec, _const_spec(fw.shape)],
        out_specs=tok_spec,
        out_shape=jax.ShapeDtypeStruct((T, D), F32),
        scratch_shapes=[pltpu.VMEM((D, PEER_TT), F32),
                        pltpu.VMEM((PEER_EC, PEER_TT), F32), pltpu.VMEM((PEER_EC, PEER_TT), F32),
                        pltpu.VMEM((PEER_EC, PEER_TT), BF16), pltpu.VMEM((PEER_EC, PEER_TT), BF16)],
        compiler_params=pltpu.CompilerParams(
            dimension_semantics=("arbitrary", "arbitrary"), vmem_limit_bytes=VMEM_LIMIT),
        name="peer_ffn",
    )(xn, thr, cc, s2, e2, u, vt, x1, fw)


def _t5_bucket(dist):
    n = np.maximum(dist, 0)
    max_exact = N_BUCKETS // 2
    large = max_exact + (np.log(np.maximum(n, 1) / max_exact) / np.log(MAX_DISTANCE / max_exact)
                         * (N_BUCKETS - max_exact)).astype(np.int32)
    large = np.minimum(large, N_BUCKETS - 1)
    return np.where(n < max_exact, n, large).astype(np.int32)


_PAIR_HEADS = [h for jj in range(Q_PER_KV) for h in (jj, Q_PER_KV + jj)]


def _band_bias(rel_bias):
    qi = np.arange(BLK)[:, None]
    kj = np.arange(2 * BLK)[None, :]
    dist = qi + BLK - kj
    in_window = (dist >= 0) & (dist < WINDOW)
    onehot = (_t5_bucket(dist)[:, :, None] == np.arange(N_BUCKETS)).astype(np.float32)
    per_head = rel_bias.astype(F32)[:, np.array(_PAIR_HEADS)]
    bias = jnp.einsum('qkb,bh->hqk', onehot, per_head, precision=lax.Precision.HIGHEST)
    bias = jnp.where(in_window[None], bias, NEG_INF)
    return bias.reshape(N_Q_HEADS * BLK, 2 * BLK)


def kernel(x, mix_norm_w, w_in, rel_bias, attn_sinks, w_branch_attn, gmlp_ln_w, gmlp_ln_b, gmlp_w_s,
           gmlp_b_s, w_branch_gmlp, w_out, ffn_norm_w, peer_w_q, peer_keys_1, peer_keys_2, peer_u,
           peer_v, final_norm_w):
    B, S, D = x.shape
    assert w_in.shape[0] == 1, "single-layer block"
    l = 0
    bias = _band_bias(rel_bias)
    causal = np.tril(np.ones((BLK, BLK), dtype=bool))
    eye = jnp.eye(PEER_HEADS, dtype=F32)
    nrow = PEER_HEADS * PEER_NKEYS

    w_q_pairs = (w_in[l][:, :ATTN_WIDTH].reshape(D, N_KV_HEADS, Q_PER_KV, HEAD_DIM)
                 .transpose(0, 2, 1, 3).reshape(D, ATTN_WIDTH))
    win = jnp.concatenate([w_q_pairs, w_in[l][:, ATTN_WIDTH:]], axis=1).astype(BF16)
    wba = (w_branch_attn[l].reshape(N_KV_HEADS, Q_PER_KV, HEAD_DIM, D)
           .transpose(1, 0, 2, 3).reshape(ATTN_WIDTH, D).astype(BF16))
    sinks = attn_sinks[l].astype(F32).reshape(N_KV_HEADS, Q_PER_KV).T.reshape(N_Q_HEADS)
    ws = jnp.where(causal[None], gmlp_w_s[l], 0).astype(BF16)
    btab = jnp.repeat(gmlp_b_s[l].astype(F32).T, GMLP_WIDTH // GMLP_GROUPS, axis=1)
    x1, xn = _mixer(x, sinks, mix_norm_w[l].reshape(1, D), win, bias, wba,
                    gmlp_ln_w[l].reshape(1, -1), gmlp_ln_b[l].reshape(1, -1), ws, btab,
                    w_branch_gmlp[l].astype(BF16), w_out[l].astype(BF16), ffn_norm_w[l].reshape(1, D))

    wqt = peer_w_q[l].reshape(D, PEER_HEADS, 2, PEER_HALF).transpose(2, 1, 3, 0).reshape(2 * nrow, D)
    k1kh = jnp.einsum('kd,hg->khgd', peer_keys_1[l], eye).reshape(nrow, nrow).astype(BF16)
    k2kh = jnp.einsum('kd,hg->khgd', peer_keys_2[l], eye).reshape(nrow, nrow).astype(BF16)
    xn2 = xn.reshape(B * S, D)
    thr, cc, s2, e2 = _peer_prep(xn2, wqt.astype(BF16), k1kh, k2kh, peer_keys_2[l].astype(BF16))
    vt = peer_v[l].reshape(PEER_N_EXPERTS // PEER_EC, PEER_EC, D).transpose(0, 2, 1).astype(BF16)
    y = _peer(xn2, thr, cc, s2, e2, peer_u[l], vt,
              x1.reshape(B * S, D), final_norm_w.reshape(1, D))
    return y.reshape(B, S, D)
```

```python
import functools

import numpy as np
import jax
import jax.numpy as jnp
from jax import lax
from jax.experimental import pallas as pl
from jax.experimental.pallas import tpu as pltpu

F32 = jnp.float32
BF16 = jnp.bfloat16

D_MODEL = 1024
HEAD_DIM = 64
N_Q_HEADS = 8
N_KV_HEADS = 2
Q_PER_KV = N_Q_HEADS // N_KV_HEADS
ATTN_WIDTH = N_Q_HEADS * HEAD_DIM
KV_WIDTH = N_KV_HEADS * HEAD_DIM
WINDOW = 128
BLK = 128
N_BUCKETS = 32
MAX_DISTANCE = 128
GMLP_GROUPS = 8
GMLP_WIDTH = 512
PEER_HEADS = 8
PEER_NKEYS = 128
PEER_N_EXPERTS = PEER_NKEYS * PEER_NKEYS
PEER_HALF = 128
PEER_TOPK = 16
EPS = 1e-6
NEG_INF = -1e30
_GELU_C0 = float(np.sqrt(2.0 / np.pi))
_GELU_C1 = 0.044715 * _GELU_C0

_C_Q, _C_K, _C_V, _C_GU, _C_GV, _C_GA, _C_GB, _C_END = 0, 512, 640, 768, 1280, 1792, 2816, 3840

MIX_TM = 512
PREP_TT = 512
PEER_TT = 512
PEER_EC = 1024
GATE_ROWS = 32
VMEM_LIMIT = 56 * 1024 * 1024


def _rms(x, w):
    return x * lax.rsqrt(jnp.mean(x * x, axis=-1, keepdims=True) + EPS) * w


def _mixer_kernel(sink_ref, x_ref, nw_ref, win_ref, bias_ref, wba_ref, lnw_ref, lnb_ref, ws_ref,
                  btab_ref, wbg_ref, wout_ref, fnw_ref, x1_ref, xn_ref,
                  q_scr, kext, vext, gu_scr, gv_scr, ya_scr, yb_scr):
    j = pl.program_id(1)
    nb = MIX_TM // BLK
    x = x_ref[0]
    h = _rms(x, nw_ref[...]).astype(BF16)

    def proj(lo, hi):
        return jnp.dot(h, win_ref[:, lo:hi], preferred_element_type=F32)

    q_scr[...] = proj(_C_Q, _C_K) * (HEAD_DIM ** -0.5)

    @pl.when(j == 0)
    def _():
        kext[0:BLK, :] = jnp.zeros((BLK, KV_WIDTH), BF16)
        vext[0:BLK, :] = jnp.zeros((BLK, KV_WIDTH), BF16)

    kext[BLK:, :] = proj(_C_K, _C_V).astype(BF16)
    vext[BLK:, :] = proj(_C_V, _C_GU).astype(BF16)
    gu_scr[...] = jax.nn.gelu(proj(_C_GU, _C_GV))
    gv = jax.nn.gelu(proj(_C_GV, _C_GA))
    mu = jnp.mean(gv, axis=-1, keepdims=True)
    gc = gv - mu
    gvn = gc * lax.rsqrt(jnp.mean(gc * gc, axis=-1, keepdims=True) + EPS)
    gv_scr[...] = (gvn * lnw_ref[...] + lnb_ref[...]).astype(BF16)

    lane = lax.broadcasted_iota(jnp.int32, (BLK, BLK), 1)
    low = lane < HEAD_DIM
    col = lax.broadcasted_iota(jnp.int32, (BLK, 2 * BLK), 1)

    def block(b, carry):
        r0 = b * BLK
        qb = q_scr[pl.ds(r0, BLK), :]
        kb = kext[pl.ds(r0, 2 * BLK), :]
        vb = vext[pl.ds(r0, 2 * BLK), :]
        parts = []
        for jj in range(4):
            p_j = qb[:, jj * BLK:(jj + 1) * BLK]
            parts.append(jnp.where(low, p_j, 0.0).astype(BF16))
            parts.append(jnp.where(low, 0.0, p_j).astype(BF16))
        qs = jnp.concatenate(parts, axis=0)
        logits = lax.dot_general(qs, kb, (((1,), (1,)), ((), ())), preferred_element_type=F32)
        left_cut = jnp.where(j * nb + b == 0, BLK, 0)
        outs = []
        for r in range(8):
            lg = logits[r * BLK:(r + 1) * BLK, :] + bias_ref[r * BLK:(r + 1) * BLK, :]
            lg = jnp.where(col < left_cut, NEG_INF, lg)
            s = sink_ref[r]
            m = jnp.maximum(jnp.max(lg, axis=-1, keepdims=True), s)
            p = jnp.exp(lg - m)
            den = jnp.sum(p, axis=-1, keepdims=True) + jnp.exp(s - m)
            pv = jnp.dot(p.astype(BF16), vb, preferred_element_type=F32)
            outs.append(pv / den)
        pairs = [jnp.where(low, outs[2 * jj], outs[2 * jj + 1]) for jj in range(4)]
        ya_scr[pl.ds(r0, BLK), :] = jnp.concatenate(pairs, axis=1).astype(BF16)

        gub = gu_scr[pl.ds(r0, BLK), :]
        vnb = gv_scr[pl.ds(r0, BLK), :]
        ys = []
        for mm in range(4):
            vblk = vnb[:, mm * BLK:(mm + 1) * BLK]
            s0 = jnp.dot(ws_ref[2 * mm], vblk, preferred_element_type=F32)
            s1 = jnp.dot(ws_ref[2 * mm + 1], vblk, preferred_element_type=F32)
            sg = jnp.where(low, s0, s1) + btab_ref[:, mm * BLK:(mm + 1) * BLK]
            ys.append(gub[:, mm * BLK:(mm + 1) * BLK] * sg)
        yb_scr[pl.ds(r0, BLK), :] = jnp.concatenate(ys, axis=1).astype(BF16)
        return carry

    for b_static in range(nb):
        block(b_static, 0)

    kext[0:BLK, :] = kext[MIX_TM:MIX_TM + BLK, :]
    vext[0:BLK, :] = vext[MIX_TM:MIX_TM + BLK, :]

    ya_p = jnp.dot(ya_scr[...], wba_ref[...], preferred_element_type=F32)
    yb_p = jnp.dot(yb_scr[...], wbg_ref[...], preferred_element_type=F32)
    merged = jax.nn.sigmoid(proj(_C_GA, _C_GB)) * ya_p + jax.nn.sigmoid(proj(_C_GB, _C_END)) * yb_p
    x1 = x + jnp.dot(merged.astype(BF16), wout_ref[...], preferred_element_type=F32)
    x1_ref[0] = x1
    xn_ref[0] = _rms(x1, fnw_ref[...]).astype(BF16)


def _const_spec(shape):
    nd = len(shape)
    return pl.BlockSpec(shape, lambda *_: (0,) * nd)


def _mixer(x, sinks, nw, win, bias, wba, lnw, lnb, ws, btab, wbg, wout, fnw):
    B, S, D = x.shape
    grid = (B, S // MIX_TM)
    tile = pl.BlockSpec((1, MIX_TM, D), lambda b, j: (b, j, 0))
    in_specs = [
        pl.BlockSpec(memory_space=pltpu.SMEM),
        tile,
        _const_spec(nw.shape), _const_spec(win.shape), _const_spec(bias.shape), _const_spec(wba.shape),
        _const_spec(lnw.shape), _const_spec(lnb.shape), _const_spec(ws.shape), _const_spec(btab.shape),
        _const_spec(wbg.shape), _const_spec(wout.shape), _const_spec(fnw.shape),
    ]
    return pl.pallas_call(
        _mixer_kernel,
        grid=grid,
        in_specs=in_specs,
        out_specs=[tile, tile],
        out_shape=[jax.ShapeDtypeStruct((B, S, D), F32), jax.ShapeDtypeStruct((B, S, D), BF16)],
        scratch_shapes=[
            pltpu.VMEM((MIX_TM, ATTN_WIDTH), F32),
            pltpu.VMEM((MIX_TM + BLK, KV_WIDTH), BF16),
            pltpu.VMEM((MIX_TM + BLK, KV_WIDTH), BF16),
            pltpu.VMEM((MIX_TM, GMLP_WIDTH), F32),
            pltpu.VMEM((MIX_TM, GMLP_WIDTH), BF16),
            pltpu.VMEM((MIX_TM, ATTN_WIDTH), BF16),
            pltpu.VMEM((MIX_TM, GMLP_WIDTH), BF16),
        ],
        compiler_params=pltpu.CompilerParams(
            dimension_semantics=("arbitrary", "arbitrary"), vmem_limit_bytes=VMEM_LIMIT),
        name="mixer",
    )(sinks, x, nw, win, bias, wba, lnw, lnb, ws, btab, wbg, wout, fnw)


def _cmpx(a, b):
    if a is None:
        return b, None
    if b is None:
        return a, None
    return jnp.maximum(a, b), jnp.minimum(a, b)


def _bitonic_sort_desc(v):
    v = list(v)
    n = len(v)
    k = 2
    while k <= n:
        jj = k // 2
        while jj >= 1:
            for i in range(n):
                l = i ^ jj
                if l > i:
                    hi, lo = _cmpx(v[i], v[l])
                    if (i & k) == 0:
                        v[i], v[l] = hi, lo
                    else:
                        v[i], v[l] = lo, hi
            jj //= 2
        k *= 2
    return v


def _merge_top(a, b):
    n = len(a)
    c = [_cmpx(a[i], b[n - 1 - i])[0] for i in range(n)]
    jj = n // 2
    while jj >= 1:
        for i in range(n):
            if (i & jj) == 0:
                c[i], c[i + jj] = _cmpx(c[i], c[i + jj])
        jj //= 2
    return c


def _top_sorted(vals, k):
    vals = list(vals)
    while len(vals) % k:
        vals.append(None)
    groups = [_bitonic_sort_desc(vals[g:g + k]) for g in range(0, len(vals), k)]
    while len(groups) > 1:
        nxt = [_merge_top(groups[g], groups[g + 1]) for g in range(0, len(groups) - 1, 2)]
        if len(groups) % 2:
            nxt.append(groups[-1])
        groups = nxt
    return groups[0]


def _max_below(vals, bound):
    m = None
    for v in vals:
        w = jnp.where(v < bound, v, -3.0e38)
        m = w if m is None else jnp.maximum(m, w)
    return m


def _prep_kernel(xn_ref, wqt_ref, k1_ref, k2kh_ref, k2_ref, thr_ref, c_ref, s2_ref, e2_ref,
                 s1_scr, s2kh_scr, s2hk_scr, b0_scr):
    nlb = PREP_TT // BLK
    xn = xn_ref[...]
    qt = lax.dot_general(wqt_ref[...], xn, (((1,), (1,)), ((), ())), preferred_element_type=F32)
    nrow = PEER_HEADS * PEER_HALF
    q1 = qt[:nrow].astype(BF16)
    q2 = qt[nrow:].astype(BF16)
    s1 = jnp.dot(k1_ref[...], q1, preferred_element_type=F32)
    s2kh = jnp.dot(k2kh_ref[...], q2, preferred_element_type=F32)
    for lb in range(nlb):
        s1_scr[lb] = s1[:, lb * BLK:(lb + 1) * BLK]
        s2kh_scr[lb] = s2kh[:, lb * BLK:(lb + 1) * BLK]
    for hh in range(PEER_HEADS):
        rows = slice(hh * PEER_NKEYS, (hh + 1) * PEER_NKEYS)
        s2h = jnp.dot(k2_ref[...], q2[hh * PEER_HALF:(hh + 1) * PEER_HALF, :],
                      preferred_element_type=F32)
        for lb in range(nlb):
            s2hk_scr[lb, rows, :] = s2h[:, lb * BLK:(lb + 1) * BLK]

    def lane_block(lb, carry):
        s1_rows = [s1_scr[lb, k * 8:(k + 1) * 8, :] for k in range(PEER_NKEYS)]
        s2_rows = [s2kh_scr[lb, k * 8:(k + 1) * 8, :] for k in range(PEER_NKEYS)]
        a = _top_sorted(s1_rows, PEER_TOPK)
        b = _top_sorted(s2_rows, PEER_TOPK)
        a_next = _max_below(s1_rows, a[PEER_TOPK - 1])
        b_next = _max_below(s2_rows, b[PEER_TOPK - 1])
        cands = [a[i] + b[jx] for i in range(PEER_TOPK) for jx in range(PEER_TOPK)
                 if (i + 1) * (jx + 1) <= PEER_TOPK]
        top = _top_sorted(cands, PEER_TOPK)
        runner_up = _max_below(cands + [a_next + b[0], a[0] + b_next], top[PEER_TOPK - 1])
        tau = 0.5 * (top[PEER_TOPK - 1] + runner_up)
        z = jnp.ones_like(tau)
        for kk in range(1, PEER_TOPK):
            z = z + jnp.exp(top[kk] - top[0])
        inv_z = 0.5 / z
        tau_rel = tau - b[0]
        for k in range(PEER_NKEYS):
            s1k = s1_scr[lb, k * 8:(k + 1) * 8, :]
            thr_ref[lb, k * 8:(k + 1) * 8, :] = tau_rel - s1k
            c_ref[lb, k * 8:(k + 1) * 8, :] = jnp.exp(s1k - a[0]) * inv_z
        b0_scr[lb] = b[0]
        return carry

    for lb_static in range(nlb):
        lane_block(lb_static, 0)

    for lb in range(nlb):
        for hh in range(PEER_HEADS):
            rows = slice(hh * PEER_NKEYS, (hh + 1) * PEER_NKEYS)
            s2_rel = s2hk_scr[lb, rows, :] - b0_scr[lb, hh:hh + 1, :]
            s2_ref[lb, rows, :] = s2_rel
            e2_ref[lb, rows, :] = jnp.exp(s2_rel)


def _peer_prep(xn, wqt, k1kh, k2kh, k2):
    T, D = xn.shape
    nrow = PEER_HEADS * PEER_NKEYS
    nlb = PREP_TT // BLK
    out_spec = pl.BlockSpec((nlb, nrow, BLK), lambda i: (i, 0, 0))
    rows_f32 = jax.ShapeDtypeStruct((T // BLK, nrow, BLK), F32)
    return pl.pallas_call(
        _prep_kernel,
        grid=(T // PREP_TT,),
        in_specs=[pl.BlockSpec((PREP_TT, D), lambda i: (i, 0)),
                  _const_spec(wqt.shape), _const_spec(k1kh.shape), _const_spec(k2kh.shape),
                  _const_spec(k2.shape)],
        out_specs=[out_spec] * 4,
        out_shape=[rows_f32] * 4,
        scratch_shapes=[pltpu.VMEM((nlb, nrow, BLK), F32), pltpu.VMEM((nlb, nrow, BLK), F32),
                        pltpu.VMEM((nlb, nrow, BLK), F32), pltpu.VMEM((nlb, PEER_HEADS, BLK), F32)],
        compiler_params=pltpu.CompilerParams(
            dimension_semantics=("arbitrary",), vmem_limit_bytes=VMEM_LIMIT),
        name="peer_prep",
    )(xn, wqt, k1kh, k2kh, k2)


def _peer_stage(c_gate, xn_ref, thr_ref, c_ref, s2_ref, e2_ref, u_ref, vt_ref, yt_scr,
                ht_w, ht_r, at_w, at_r, *, first_matmul=True, gate=True, second_matmul=True):
    n_i1 = PEER_EC // PEER_NKEYS
    nlb = PEER_TT // BLK
    if second_matmul:
        yt_scr[...] += jnp.dot(vt_ref[0], at_r[...], preferred_element_type=F32)
    if first_matmul:
        ht_w[...] = lax.dot_general(u_ref[...], xn_ref[...], (((1,), (1,)), ((), ())),
                                    preferred_element_type=F32)
    if not gate:
        return
    row0 = c_gate * (n_i1 * PEER_HEADS)
    for lb in range(nlb):
        lanes = slice(lb * BLK, (lb + 1) * BLK)
        for rb in range(PEER_NKEYS // GATE_ROWS):
            gates = [jnp.zeros((GATE_ROWS, BLK), F32) for _ in range(n_i1)]
            for hh in range(PEER_HEADS):
                rows = slice(hh * PEER_NKEYS + rb * GATE_ROWS, hh * PEER_NKEYS + (rb + 1) * GATE_ROWS)
                s2 = s2_ref[lb, rows, :]
                e2 = e2_ref[lb, rows, :]
                for ii in range(n_i1):
                    r = row0 + ii * PEER_HEADS + hh
                    thr = thr_ref[lb, pl.ds(r, 1), :]
                    cc = c_ref[lb, pl.ds(r, 1), :]
                    gates[ii] = gates[ii] + jnp.where(s2 >= thr, e2, 0.0) * cc
            for ii in range(n_i1):
                rows = slice(ii * PEER_NKEYS + rb * GATE_ROWS, ii * PEER_NKEYS + (rb + 1) * GATE_ROWS)
                x = ht_r[rows, lanes]
                t = jnp.tanh(x * (_GELU_C0 + _GELU_C1 * (x * x)))
                at_w[rows, lanes] = ((x + x * t) * gates[ii]).astype(BF16)


def _peer_kernel(xn_ref, thr_ref, c_ref, s2_ref, e2_ref, u_ref, vt_ref, x1_ref, fw_ref, out_ref,
                 yt_scr, ht_a, ht_b, at_a, at_b):
    s = pl.program_id(1)
    n_chunks = PEER_N_EXPERTS // PEER_EC

    @pl.when(s == 0)
    def _():
        yt_scr[...] = jnp.zeros_like(yt_scr)

    c_gate = jnp.clip(s - 1, 0, n_chunks - 1)
    stage = functools.partial(_peer_stage, c_gate, xn_ref, thr_ref, c_ref, s2_ref, e2_ref, u_ref,
                              vt_ref, yt_scr)
    even = lax.rem(s, 2) == 0
    inner = jnp.logical_and(s >= 2, s < n_chunks)

    @pl.when(jnp.logical_and(even, inner))
    def _():
        stage(ht_a, ht_b, at_b, at_a)

    @pl.when(jnp.logical_and(jnp.logical_not(even), inner))
    def _():
        stage(ht_b, ht_a, at_a, at_b)

    @pl.when(s == 0)
    def _():
        stage(ht_a, ht_b, at_b, at_a, gate=False, second_matmul=False)

    @pl.when(s == 1)
    def _():
        stage(ht_b, ht_a, at_a, at_b, second_matmul=False)

    @pl.when(s == n_chunks)
    def _():
        stage(ht_a, ht_b, at_b, at_a, first_matmul=False)

    @pl.when(s == n_chunks + 1)
    def _():
        stage(ht_b, ht_a, at_a, at_b, first_matmul=False, gate=False)

    @pl.when(s == pl.num_programs(1) - 1)
    def _():
        xo = x1_ref[...] + yt_scr[...].T
        out_ref[...] = _rms(xo, fw_ref[...])


def _peer(xn, thr, cc, s2, e2, u, vt, x1, fw):
    T, D = xn.shape
    nrow = PEER_HEADS * PEER_NKEYS
    nlb = PEER_TT // BLK
    n_chunks = PEER_N_EXPERTS // PEER_EC
    assert n_chunks % 2 == 0 and n_chunks >= 4
    prep_spec = pl.BlockSpec((nlb, nrow, BLK), lambda i, s: (i, 0, 0))
    tok_spec = pl.BlockSpec((PEER_TT, D), lambda i, s: (i, 0))
    return pl.pallas_call(
        _peer_kernel,
        grid=(T // PEER_TT, n_chunks + 2),
        in_specs=[tok_spec, prep_spec, prep_spec, prep_spec, prep_spec,
                  pl.BlockSpec((PEER_EC, D), lambda i, s: (jnp.minimum(s, n_chunks - 1), 0)),
                  pl.BlockSpec((1, D, PEER_EC), lambda i, s: (jnp.clip(s - 2, 0, n_chunks - 1), 0, 0)),
                  tok_spec, _const_spec(fw.shape)],
        out_specs=tok_spec,
        out_shape=jax.ShapeDtypeStruct((T, D), F32),
        scratch_shapes=[pltpu.VMEM((D, PEER_TT), F32),
                        pltpu.VMEM((PEER_EC, PEER_TT), F32), pltpu.VMEM((PEER_EC, PEER_TT), F32),
                        pltpu.VMEM((PEER_EC, PEER_TT), BF16), pltpu.VMEM((PEER_EC, PEER_TT), BF16)],
        compiler_params=pltpu.CompilerParams(
            dimension_semantics=("arbitrary", "arbitrary"), vmem_limit_bytes=VMEM_LIMIT),
        name="peer_ffn",
    )(xn, thr, cc, s2, e2, u, vt, x1, fw)


def _t5_bucket(dist):
    n = np.maximum(dist, 0)
    max_exact = N_BUCKETS // 2
    large = max_exact + (np.log(np.maximum(n, 1) / max_exact) / np.log(MAX_DISTANCE / max_exact)
                         * (N_BUCKETS - max_exact)).astype(np.int32)
    large = np.minimum(large, N_BUCKETS - 1)
    return np.where(n < max_exact, n, large).astype(np.int32)


_PAIR_HEADS = [h for jj in range(Q_PER_KV) for h in (jj, Q_PER_KV + jj)]


def _band_bias(rel_bias):
    qi = np.arange(BLK)[:, None]
    kj = np.arange(2 * BLK)[None, :]
    dist = qi + BLK - kj
    in_window = (dist >= 0) & (dist < WINDOW)
    onehot = (_t5_bucket(dist)[:, :, None] == np.arange(N_BUCKETS)).astype(np.float32)
    per_head = rel_bias.astype(F32)[:, np.array(_PAIR_HEADS)]
    bias = jnp.einsum('qkb,bh->hqk', onehot, per_head, precision=lax.Precision.HIGHEST)
    bias = jnp.where(in_window[None], bias, NEG_INF)
    return bias.reshape(N_Q_HEADS * BLK, 2 * BLK)


def kernel(x, mix_norm_w, w_in, rel_bias, attn_sinks, w_branch_attn, gmlp_ln_w, gmlp_ln_b, gmlp_w_s,
           gmlp_b_s, w_branch_gmlp, w_out, ffn_norm_w, peer_w_q, peer_keys_1, peer_keys_2, peer_u,
           peer_v, final_norm_w):
    B, S, D = x.shape
    assert w_in.shape[0] == 1, "single-layer block"
    l = 0
    bias = _band_bias(rel_bias)
    causal = np.tril(np.ones((BLK, BLK), dtype=bool))
    eye = jnp.eye(PEER_HEADS, dtype=F32)
    nrow = PEER_HEADS * PEER_NKEYS

    w_q_pairs = (w_in[l][:, :ATTN_WIDTH].reshape(D, N_KV_HEADS, Q_PER_KV, HEAD_DIM)
                 .transpose(0, 2, 1, 3).reshape(D, ATTN_WIDTH))
    win = jnp.concatenate([w_q_pairs, w_in[l][:, ATTN_WIDTH:]], axis=1).astype(BF16)
    wba = (w_branch_attn[l].reshape(N_KV_HEADS, Q_PER_KV, HEAD_DIM, D)
           .transpose(1, 0, 2, 3).reshape(ATTN_WIDTH, D).astype(BF16))
    sinks = attn_sinks[l].astype(F32).reshape(N_KV_HEADS, Q_PER_KV).T.reshape(N_Q_HEADS)
    ws = jnp.where(causal[None], gmlp_w_s[l], 0).astype(BF16)
    btab = jnp.repeat(gmlp_b_s[l].astype(F32).T, GMLP_WIDTH // GMLP_GROUPS, axis=1)
    x1, xn = _mixer(x, sinks, mix_norm_w[l].reshape(1, D), win, bias, wba,
                    gmlp_ln_w[l].reshape(1, -1), gmlp_ln_b[l].reshape(1, -1), ws, btab,
                    w_branch_gmlp[l].astype(BF16), w_out[l].astype(BF16), ffn_norm_w[l].reshape(1, D))

    wqt = peer_w_q[l].reshape(D, PEER_HEADS, 2, PEER_HALF).transpose(2, 1, 3, 0).reshape(2 * nrow, D)
    k1kh = jnp.einsum('kd,hg->khgd', peer_keys_1[l], eye).reshape(nrow, nrow).astype(BF16)
    k2kh = jnp.einsum('kd,hg->khgd', peer_keys_2[l], eye).reshape(nrow, nrow).astype(BF16)
    xn2 = xn.reshape(B * S, D)
    thr, cc, s2, e2 = _peer_prep(xn2, wqt.astype(BF16), k1kh, k2kh, peer_keys_2[l].astype(BF16))
    vt = peer_v[l].reshape(PEER_N_EXPERTS // PEER_EC, PEER_EC, D).transpose(0, 2, 1).astype(BF16)
    y = _peer(xn2, thr, cc, s2, e2, peer_u[l], vt,
              x1.reshape(B * S, D), final_norm_w.reshape(1, D))
    return y.reshape(B, S, D)
```

```python
import functools

import numpy as np
import jax
import jax.numpy as jnp
from jax import lax
from jax.experimental import pallas as pl
from jax.experimental.pallas import tpu as pltpu

F32 = jnp.float32
BF16 = jnp.bfloat16

D_MODEL = 1024
HEAD_DIM = 64
N_Q_HEADS = 8
N_KV_HEADS = 2
Q_PER_KV = N_Q_HEADS // N_KV_HEADS
ATTN_WIDTH = N_Q_HEADS * HEAD_DIM
KV_WIDTH = N_KV_HEADS * HEAD_DIM
WINDOW = 128
BLK = 128
N_BUCKETS = 32
MAX_DISTANCE = 128
GMLP_GROUPS = 8
GMLP_WIDTH = 512
PEER_HEADS = 8
PEER_NKEYS = 128
PEER_N_EXPERTS = PEER_NKEYS * PEER_NKEYS
PEER_HALF = 128
PEER_TOPK = 16
EPS = 1e-6
NEG_INF = -1e30
_GELU_C0 = float(np.sqrt(2.0 / np.pi))
_GELU_C1 = 0.044715 * _GELU_C0

_C_Q, _C_K, _C_V, _C_GU, _C_GV, _C_GA, _C_GB, _C_END = 0, 512, 640, 768, 1280, 1792, 2816, 3840

MIX_TM = 512
PREP_TT = 512
PEER_TT = 512
PEER_EC = 1024
GATE_ROWS = 32
VMEM_LIMIT = 56 * 1024 * 1024


def _rms(x, w):
    return x * lax.rsqrt(jnp.mean(x * x, axis=-1, keepdims=True) + EPS) * w


def _mixer_kernel(sink_ref, x_ref, nw_ref, win_ref, bias_ref, wba_ref, lnw_ref, lnb_ref, ws_ref,
                  btab_ref, wbg_ref, wout_ref, fnw_ref, x1_ref, xn_ref,
                  q_scr, kext, vext, gu_scr, gv_scr, ya_scr, yb_scr):
    j = pl.program_id(1)
    nb = MIX_TM // BLK
    x = x_ref[0]
    h = _rms(x, nw_ref[...]).astype(BF16)

    def proj(lo, hi):
        return jnp.dot(h, win_ref[:, lo:hi], preferred_element_type=F32)

    q_scr[...] = proj(_C_Q, _C_K) * (HEAD_DIM ** -0.5)

    @pl.when(j == 0)
    def _():
        kext[0:BLK, :] = jnp.zeros((BLK, KV_WIDTH), BF16)
        vext[0:BLK, :] = jnp.zeros((BLK, KV_WIDTH), BF16)

    kext[BLK:, :] = proj(_C_K, _C_V).astype(BF16)
    vext[BLK:, :] = proj(_C_V, _C_GU).astype(BF16)
    gu_scr[...] = jax.nn.gelu(proj(_C_GU, _C_GV))
    gv = jax.nn.gelu(proj(_C_GV, _C_GA))
    mu = jnp.mean(gv, axis=-1, keepdims=True)
    gc = gv - mu
    gvn = gc * lax.rsqrt(jnp.mean(gc * gc, axis=-1, keepdims=True) + EPS)
    gv_scr[...] = (gvn * lnw_ref[...] + lnb_ref[...]).astype(BF16)

    lane = lax.broadcasted_iota(jnp.int32, (BLK, BLK), 1)
    low = lane < HEAD_DIM
    col = lax.broadcasted_iota(jnp.int32, (BLK, 2 * BLK), 1)

    def block(b, carry):
        r0 = b * BLK
        qb = q_scr[pl.ds(r0, BLK), :]
        kb = kext[pl.ds(r0, 2 * BLK), :]
        vb = vext[pl.ds(r0, 2 * BLK), :]
        parts = []
        for jj in range(4):
            p_j = qb[:, jj * BLK:(jj + 1) * BLK]
            parts.append(jnp.where(low, p_j, 0.0).astype(BF16))
            parts.append(jnp.where(low, 0.0, p_j).astype(BF16))
        qs = jnp.concatenate(parts, axis=0)
        logits = lax.dot_general(qs, kb, (((1,), (1,)), ((), ())), preferred_element_type=F32)
        left_cut = jnp.where(j * nb + b == 0, BLK, 0)
        outs = []
        for r in range(8):
            lg = logits[r * BLK:(r + 1) * BLK, :] + bias_ref[r * BLK:(r + 1) * BLK, :]
            lg = jnp.where(col < left_cut, NEG_INF, lg)
            s = sink_ref[r]
            m = jnp.maximum(jnp.max(lg, axis=-1, keepdims=True), s)
            p = jnp.exp(lg - m)
            den = jnp.sum(p, axis=-1, keepdims=True) + jnp.exp(s - m)
            pv = jnp.dot(p.astype(BF16), vb, preferred_element_type=F32)
            outs.append(pv / den)
        pairs = [jnp.where(low, outs[2 * jj], outs[2 * jj + 1]) for jj in range(4)]
        ya_scr[pl.ds(r0, BLK), :] = jnp.concatenate(pairs, axis=1).astype(BF16)

        gub = gu_scr[pl.ds(r0, BLK), :]
        vnb = gv_scr[pl.ds(r0, BLK), :]
        ys = []
        for mm in range(4):
            vblk = vnb[:, mm * BLK:(mm + 1) * BLK]
            s0 = jnp.dot(ws_ref[2 * mm], vblk, preferred_element_type=F32)
            s1 = jnp.dot(ws_ref[2 * mm + 1], vblk, preferred_element_type=F32)
            sg = jnp.where(low, s0, s1) + btab_ref[:, mm * BLK:(mm + 1) * BLK]
            ys.append(gub[:, mm * BLK:(mm + 1) * BLK] * sg)
        yb_scr[pl.ds(r0, BLK), :] = jnp.concatenate(ys, axis=1).astype(BF16)
        return carry

    for b_static in range(nb):
        block(b_static, 0)

    kext[0:BLK, :] = kext[MIX_TM:MIX_TM + BLK, :]
    vext[0:BLK, :] = vext[MIX_TM:MIX_TM + BLK, :]

    ya_p = jnp.dot(ya_scr[...], wba_ref[...], preferred_element_type=F32)
    yb_p = jnp.dot(yb_scr[...], wbg_ref[...], preferred_element_type=F32)
    merged = jax.nn.sigmoid(proj(_C_GA, _C_GB)) * ya_p + jax.nn.sigmoid(proj(_C_GB, _C_END)) * yb_p
    x1 = x + jnp.dot(merged.astype(BF16), wout_ref[...], preferred_element_type=F32)
    x1_ref[0] = x1
    xn_ref[0] = _rms(x1, fnw_ref[...]).astype(BF16)


def _const_spec(shape):
    nd = len(shape)
    return pl.BlockSpec(shape, lambda *_: (0,) * nd, pipeline_mode=pl.Buffered(1))


def _mixer(x, sinks, nw, win, bias, wba, lnw, lnb, ws, btab, wbg, wout, fnw):
    B, S, D = x.shape
    grid = (B, S // MIX_TM)
    tile = pl.BlockSpec((1, MIX_TM, D), lambda b, j: (b, j, 0))
    in_specs = [
        pl.BlockSpec(memory_space=pltpu.SMEM),
        tile,
        _const_spec(nw.shape), _const_spec(win.shape), _const_spec(bias.shape), _const_spec(wba.shape),
        _const_spec(lnw.shape), _const_spec(lnb.shape), _const_spec(ws.shape), _const_spec(btab.shape),
        _const_spec(wbg.shape), _const_spec(wout.shape), _const_spec(fnw.shape),
    ]
    return pl.pallas_call(
        _mixer_kernel,
        grid=grid,
        in_specs=in_specs,
        out_specs=[tile, tile],
        out_shape=[jax.ShapeDtypeStruct((B, S, D), F32), jax.ShapeDtypeStruct((B, S, D), BF16)],
        scratch_shapes=[
            pltpu.VMEM((MIX_TM, ATTN_WIDTH), F32),
            pltpu.VMEM((MIX_TM + BLK, KV_WIDTH), BF16),
            pltpu.VMEM((MIX_TM + BLK, KV_WIDTH), BF16),
            pltpu.VMEM((MIX_TM, GMLP_WIDTH), F32),
            pltpu.VMEM((MIX_TM, GMLP_WIDTH), BF16),
            pltpu.VMEM((MIX_TM, ATTN_WIDTH), BF16),
            pltpu.VMEM((MIX_TM, GMLP_WIDTH), BF16),
        ],
        compiler_params=pltpu.CompilerParams(
            dimension_semantics=("arbitrary", "arbitrary"), vmem_limit_bytes=VMEM_LIMIT),
        name="mixer",
    )(sinks, x, nw, win, bias, wba, lnw, lnb, ws, btab, wbg, wout, fnw)


def _cmpx(a, b):
    if a is None:
        return b, None
    if b is None:
        return a, None
    return jnp.maximum(a, b), jnp.minimum(a, b)


def _bitonic_sort_desc(v):
    v = list(v)
    n = len(v)
    k = 2
    while k <= n:
        jj = k // 2
        while jj >= 1:
            for i in range(n):
                l = i ^ jj
                if l > i:
                    hi, lo = _cmpx(v[i], v[l])
                    if (i & k) == 0:
                        v[i], v[l] = hi, lo
                    else:
                        v[i], v[l] = lo, hi
            jj //= 2
        k *= 2
    return v


def _merge_top(a, b):
    n = len(a)
    c = [_cmpx(a[i], b[n - 1 - i])[0] for i in range(n)]
    jj = n // 2
    while jj >= 1:
        for i in range(n):
            if (i & jj) == 0:
                c[i], c[i + jj] = _cmpx(c[i], c[i + jj])
        jj //= 2
    return c


def _top_sorted(vals, k):
    vals = list(vals)
    while len(vals) % k:
        vals.append(None)
    groups = [_bitonic_sort_desc(vals[g:g + k]) for g in range(0, len(vals), k)]
    while len(groups) > 1:
        nxt = [_merge_top(groups[g], groups[g + 1]) for g in range(0, len(groups) - 1, 2)]
        if len(groups) % 2:
            nxt.append(groups[-1])
        groups = nxt
    return groups[0]


def _max_below(vals, bound):
    m = None
    for v in vals:
        w = jnp.where(v < bound, v, -3.0e38)
        m = w if m is None else jnp.maximum(m, w)
    return m


def _prep_kernel(xn_ref, wqt_ref, k1_ref, k2kh_ref, k2_ref, thr_ref, c_ref, s2_ref, e2_ref,
                 s1_scr, s2kh_scr, s2hk_scr, b0_scr):
    nlb = PREP_TT // BLK
    xn = xn_ref[...]
    qt = lax.dot_general(wqt_ref[...], xn, (((1,), (1,)), ((), ())), preferred_element_type=F32)
    nrow = PEER_HEADS * PEER_HALF
    q1 = qt[:nrow].astype(BF16)
    q2 = qt[nrow:].astype(BF16)
    s1 = jnp.dot(k1_ref[...], q1, preferred_element_type=F32)
    s2kh = jnp.dot(k2kh_ref[...], q2, preferred_element_type=F32)
    for lb in range(nlb):
        s1_scr[lb] = s1[:, lb * BLK:(lb + 1) * BLK]
        s2kh_scr[lb] = s2kh[:, lb * BLK:(lb + 1) * BLK]
    for hh in range(PEER_HEADS):
        rows = slice(hh * PEER_NKEYS, (hh + 1) * PEER_NKEYS)
        s2h = jnp.dot(k2_ref[...], q2[hh * PEER_HALF:(hh + 1) * PEER_HALF, :],
                      preferred_element_type=F32)
        for lb in range(nlb):
            s2hk_scr[lb, rows, :] = s2h[:, lb * BLK:(lb + 1) * BLK]

    def lane_block(lb, carry):
        s1_rows = [s1_scr[lb, k * 8:(k + 1) * 8, :] for k in range(PEER_NKEYS)]
        s2_rows = [s2kh_scr[lb, k * 8:(k + 1) * 8, :] for k in range(PEER_NKEYS)]
        a = _top_sorted(s1_rows, PEER_TOPK)
        b = _top_sorted(s2_rows, PEER_TOPK)
        a_next = _max_below(s1_rows, a[PEER_TOPK - 1])
        b_next = _max_below(s2_rows, b[PEER_TOPK - 1])
        cands = [a[i] + b[jx] for i in range(PEER_TOPK) for jx in range(PEER_TOPK)
                 if (i + 1) * (jx + 1) <= PEER_TOPK]
        top = _top_sorted(cands, PEER_TOPK)
        runner_up = _max_below(cands + [a_next + b[0], a[0] + b_next], top[PEER_TOPK - 1])
        tau = 0.5 * (top[PEER_TOPK - 1] + runner_up)
        z = jnp.ones_like(tau)
        for kk in range(1, PEER_TOPK):
            z = z + jnp.exp(top[kk] - top[0])
        inv_z = 0.5 / z
        tau_rel = tau - b[0]
        for k in range(PEER_NKEYS):
            s1k = s1_scr[lb, k * 8:(k + 1) * 8, :]
            thr_ref[lb, k * 8:(k + 1) * 8, :] = tau_rel - s1k
            c_ref[lb, k * 8:(k + 1) * 8, :] = jnp.exp(s1k - a[0]) * inv_z
        b0_scr[lb] = b[0]
        return carry

    for lb_static in range(nlb):
        lane_block(lb_static, 0)

    for lb in range(nlb):
        for hh in range(PEER_HEADS):
            rows = slice(hh * PEER_NKEYS, (hh + 1) * PEER_NKEYS)
            s2_rel = s2hk_scr[lb, rows, :] - b0_scr[lb, hh:hh + 1, :]
            s2_ref[lb, rows, :] = s2_rel
            e2_ref[lb, rows, :] = jnp.exp(s2_rel)


def _peer_prep(xn, wqt, k1kh, k2kh, k2):
    T, D = xn.shape
    nrow = PEER_HEADS * PEER_NKEYS
    nlb = PREP_TT // BLK
    out_spec = pl.BlockSpec((nlb, nrow, BLK), lambda i: (i, 0, 0))
    rows_f32 = jax.ShapeDtypeStruct((T // BLK, nrow, BLK), F32)
    return pl.pallas_call(
        _prep_kernel,
        grid=(T // PREP_TT,),
        in_specs=[pl.BlockSpec((PREP_TT, D), lambda i: (i, 0)),
                  _const_spec(wqt.shape), _const_spec(k1kh.shape), _const_spec(k2kh.shape),
                  _const_spec(k2.shape)],
        out_specs=[out_spec] * 4,
        out_shape=[rows_f32] * 4,
        scratch_shapes=[pltpu.VMEM((nlb, nrow, BLK), F32), pltpu.VMEM((nlb, nrow, BLK), F32),
                        pltpu.VMEM((nlb, nrow, BLK), F32), pltpu.VMEM((nlb, PEER_HEADS, BLK), F32)],
        compiler_params=pltpu.CompilerParams(
            dimension_semantics=("arbitrary",), vmem_limit_bytes=VMEM_LIMIT),
        name="peer_prep",
    )(xn, wqt, k1kh, k2kh, k2)


def _peer_stage(c_gate, xn_ref, thr_ref, c_ref, s2_ref, e2_ref, u_ref, vt_ref, yt_scr,
                ht_w, ht_r, at_w, at_r, *, first_matmul=True, gate=True, second_matmul=True):
    n_i1 = PEER_EC // PEER_NKEYS
    nlb = PEER_TT // BLK
    if second_matmul:
        yt_scr[...] += jnp.dot(vt_ref[0], at_r[...], preferred_element_type=F32)
    if first_matmul:
        ht_w[...] = lax.dot_general(u_ref[...], xn_ref[...], (((1,), (1,)), ((), ())),
                                    preferred_element_type=F32)
    if not gate:
        return
    row0 = c_gate * (n_i1 * PEER_HEADS)
    for lb in range(nlb):
        lanes = slice(lb * BLK, (lb + 1) * BLK)
        for rb in range(PEER_NKEYS // GATE_ROWS):
            gates = [jnp.zeros((GATE_ROWS, BLK), F32) for _ in range(n_i1)]
            for hh in range(PEER_HEADS):
                rows = slice(hh * PEER_NKEYS + rb * GATE_ROWS, hh * PEER_NKEYS + (rb + 1) * GATE_ROWS)
                s2 = s2_ref[lb, rows, :]
                e2 = e2_ref[lb, rows, :]
                for ii in range(n_i1):
                    r = row0 + ii * PEER_HEADS + hh
                    thr = thr_ref[lb, pl.ds(r, 1), :]
                    cc = c_ref[lb, pl.ds(r, 1), :]
                    gates[ii] = gates[ii] + jnp.where(s2 >= thr, e2, 0.0) * cc
            for ii in range(n_i1):
                rows = slice(ii * PEER_NKEYS + rb * GATE_ROWS, ii * PEER_NKEYS + (rb + 1) * GATE_ROWS)
                x = ht_r[rows, lanes]
                t = jnp.tanh(x * (_GELU_C0 + _GELU_C1 * (x * x)))
                at_w[rows, lanes] = ((x + x * t) * gates[ii]).astype(BF16)


def _peer_kernel(xn_ref, thr_ref, c_ref, s2_ref, e2_ref, u_ref, vt_ref, x1_ref, fw_ref, out_ref,
                 yt_scr, ht_a, ht_b, at_a, at_b):
    s = pl.program_id(1)
    n_chunks = PEER_N_EXPERTS // PEER_EC

    @pl.when(s == 0)
    def _():
        yt_scr[...] = jnp.zeros_like(yt_scr)

    c_gate = jnp.clip(s - 1, 0, n_chunks - 1)
    stage = functools.partial(_peer_stage, c_gate, xn_ref, thr_ref, c_ref, s2_ref, e2_ref, u_ref,
                              vt_ref, yt_scr)
    even = lax.rem(s, 2) == 0
    inner = jnp.logical_and(s >= 2, s < n_chunks)

    @pl.when(jnp.logical_and(even, inner))
    def _():
        stage(ht_a, ht_b, at_b, at_a)

    @pl.when(jnp.logical_and(jnp.logical_not(even), inner))
    def _():
        stage(ht_b, ht_a, at_a, at_b)

    @pl.when(s == 0)
    def _():
        stage(ht_a, ht_b, at_b, at_a, gate=False, second_matmul=False)

    @pl.when(s == 1)
    def _():
        stage(ht_b, ht_a, at_a, at_b, second_matmul=False)

    @pl.when(s == n_chunks)
    def _():
        stage(ht_a, ht_b, at_b, at_a, first_matmul=False)

    @pl.when(s == n_chunks + 1)
    def _():
        stage(ht_b, ht_a, at_a, at_b, first_matmul=False, gate=False)

    @pl.when(s == pl.num_programs(1) - 1)
    def _():
        xo = x1_ref[...] + yt_scr[...].T
        out_ref[...] = _rms(xo, fw_ref[...])


def _peer(xn, thr, cc, s2, e2, u, vt, x1, fw):
    T, D = xn.shape
    nrow = PEER_HEADS * PEER_NKEYS
    nlb = PEER_TT // BLK
    n_chunks = PEER_N_EXPERTS // PEER_EC
    assert n_chunks % 2 == 0 and n_chunks >= 4
    prep_spec = pl.BlockSpec((nlb, nrow, BLK), lambda i, s: (i, 0, 0))
    tok_spec = pl.BlockSpec((PEER_TT, D), lambda i, s: (i, 0))
    return pl.pallas_call(
        _peer_kernel,
        grid=(T // PEER_TT, n_chunks + 2),
        in_specs=[tok_spec, prep_spec, prep_spec, prep_spec, prep_spec,
                  pl.BlockSpec((PEER_EC, D), lambda i, s: (jnp.minimum(s, n_chunks - 1), 0)),
                  pl.BlockSpec((1, D, PEER_EC), lambda i, s: (jnp.clip(s - 2, 0, n_chunks - 1), 0, 0)),
                  tok_spec, _const_spec(fw.shape)],
        out_specs=tok_spec,
        out_shape=jax.ShapeDtypeStruct((T, D), F32),
        scratch_shapes=[pltpu.VMEM((D, PEER_TT), F32),
                        pltpu.VMEM((PEER_EC, PEER_TT), F32), pltpu.VMEM((PEER_EC, PEER_TT), F32),
                        pltpu.VMEM((PEER_EC, PEER_TT), BF16), pltpu.VMEM((PEER_EC, PEER_TT), BF16)],
        compiler_params=pltpu.CompilerParams(
            dimension_semantics=("arbitrary", "arbitrary"), vmem_limit_bytes=VMEM_LIMIT),
        name="peer_ffn",
    )(xn, thr, cc, s2, e2, u, vt, x1, fw)


def _t5_bucket(dist):
    n = np.maximum(dist, 0)
    max_exact = N_BUCKETS // 2
    large = max_exact + (np.log(np.maximum(n, 1) / max_exact) / np.log(MAX_DISTANCE / max_exact)
                         * (N_BUCKETS - max_exact)).astype(np.int32)
    large = np.minimum(large, N_BUCKETS - 1)
    return np.where(n < max_exact, n, large).astype(np.int32)


_PAIR_HEADS = [h for jj in range(Q_PER_KV) for h in (jj, Q_PER_KV + jj)]


def _band_bias(rel_bias):
    qi = np.arange(BLK)[:, None]
    kj = np.arange(2 * BLK)[None, :]
    dist = qi + BLK - kj
    in_window = (dist >= 0) & (dist < WINDOW)
    onehot = (_t5_bucket(dist)[:, :, None] == np.arange(N_BUCKETS)).astype(np.float32)
    per_head = rel_bias.astype(F32)[:, np.array(_PAIR_HEADS)]
    bias = jnp.einsum('qkb,bh->hqk', onehot, per_head, precision=lax.Precision.HIGHEST)
    bias = jnp.where(in_window[None], bias, NEG_INF)
    return bias.reshape(N_Q_HEADS * BLK, 2 * BLK)


def kernel(x, mix_norm_w, w_in, rel_bias, attn_sinks, w_branch_attn, gmlp_ln_w, gmlp_ln_b, gmlp_w_s,
           gmlp_b_s, w_branch_gmlp, w_out, ffn_norm_w, peer_w_q, peer_keys_1, peer_keys_2, peer_u,
           peer_v, final_norm_w):
    B, S, D = x.shape
    assert w_in.shape[0] == 1, "single-layer block"
    l = 0
    bias = _band_bias(rel_bias)
    causal = np.tril(np.ones((BLK, BLK), dtype=bool))
    eye = jnp.eye(PEER_HEADS, dtype=F32)
    nrow = PEER_HEADS * PEER_NKEYS

    w_q_pairs = (w_in[l][:, :ATTN_WIDTH].reshape(D, N_KV_HEADS, Q_PER_KV, HEAD_DIM)
                 .transpose(0, 2, 1, 3).reshape(D, ATTN_WIDTH))
    win = jnp.concatenate([w_q_pairs, w_in[l][:, ATTN_WIDTH:]], axis=1).astype(BF16)
    wba = (w_branch_attn[l].reshape(N_KV_HEADS, Q_PER_KV, HEAD_DIM, D)
           .transpose(1, 0, 2, 3).reshape(ATTN_WIDTH, D).astype(BF16))
    sinks = attn_sinks[l].astype(F32).reshape(N_KV_HEADS, Q_PER_KV).T.reshape(N_Q_HEADS)
    ws = jnp.where(causal[None], gmlp_w_s[l], 0).astype(BF16)
    btab = jnp.repeat(gmlp_b_s[l].astype(F32).T, GMLP_WIDTH // GMLP_GROUPS, axis=1)
    x1, xn = _mixer(x, sinks, mix_norm_w[l].reshape(1, D), win, bias, wba,
                    gmlp_ln_w[l].reshape(1, -1), gmlp_ln_b[l].reshape(1, -1), ws, btab,
                    w_branch_gmlp[l].astype(BF16), w_out[l].astype(BF16), ffn_norm_w[l].reshape(1, D))

    wqt = peer_w_q[l].reshape(D, PEER_HEADS, 2, PEER_HALF).transpose(2, 1, 3, 0).reshape(2 * nrow, D)
    k1kh = jnp.einsum('kd,hg->khgd', peer_keys_1[l], eye).reshape(nrow, nrow).astype(BF16)
    k2kh = jnp.einsum('kd,hg->khgd', peer_keys_2[l], eye).reshape(nrow, nrow).astype(BF16)
    xn2 = xn.reshape(B * S, D)
    thr, cc, s2, e2 = _peer_prep(xn2, wqt.astype(BF16), k1kh, k2kh, peer_keys_2[l].astype(BF16))
    vt = peer_v[l].reshape(PEER_N_EXPERTS // PEER_EC, PEER_EC, D).transpose(0, 2, 1).astype(BF16)
    y = _peer(xn2, thr, cc, s2, e2, peer_u[l], vt,
              x1.reshape(B * S, D), final_norm_w.reshape(1, D))
    return y.reshape(B, S, D)
```
